```python
import math
import jax, jax.numpy as jnp
from jax import lax
import numpy as np

D_MODEL = 2048
BATCH = 2
SEQ = 16384
DEPTH = 2

F32 = jnp.float32
CHUNK = 64
N_META = 16
CONV_WIDTH = 3
D_CONV = 1536
S5_CH = 16
S5_STATE = 64
D_SSM = 512
S5_GROUPS = D_SSM // S5_CH
MLSTM_HEADS = 4
MLSTM_DH = 256
D_MLSTM = MLSTM_HEADS * MLSTM_DH
DSA_HEADS = 8
DSA_DH = 128
D_DSA = DSA_HEADS * DSA_DH
IDX_HEADS = 8
IDX_DH = 64
TOPK_MAX = 256
Q_BLOCK = 128
NUM_BUCKETS = 32
MAX_DISTANCE = 128
D_FF = 5632
N_EXPERTS = 8
TOP_K = 2
D_EXPERT = 5632
MOE_BLOCK = 512
ALPHA = (2 * DEPTH) ** 0.25
BETA = (8 * DEPTH) ** -0.25
LN_EPS = 1e-5
N_EVEN = (DEPTH + 1) // 2
N_ODD = DEPTH // 2
D_IN_EVEN = 3 * D_CONV + D_SSM
D_IN_ODD = 4 * D_MLSTM + 2 * MLSTM_HEADS + 3 * D_DSA + IDX_HEADS * IDX_DH + IDX_DH + IDX_HEADS

kernel_name = "hybrid_conv_s5_mlstm_dsa_moe_trunk"


def layer_norm(x, g, b):
    xf = x.astype(F32)
    mu = xf.mean(-1, keepdims=True)
    var = jnp.square(xf - mu).mean(-1, keepdims=True)
    return ((xf - mu) * lax.rsqrt(var + LN_EPS) * g.astype(F32) + b.astype(F32)).astype(x.dtype)


def chunk_id(pos):
    return jnp.where(pos < N_META, 0, 1 + (pos - N_META) // CHUNK)


def t5_bucket(rel):
    half = NUM_BUCKETS // 2
    max_exact = half // 2
    ret = jnp.where(rel > 0, half, 0)
    n = jnp.abs(rel)
    nf = jnp.maximum(n, 1).astype(F32)
    large = max_exact + (jnp.log(nf / max_exact) / math.log(MAX_DISTANCE / max_exact) * (half - max_exact)).astype(jnp.int32)
    large = jnp.minimum(large, half - 1)
    return ret + jnp.where(n < max_exact, n, large)


def short_conv_mixer(b_gate, c_gate, xv, conv_w):
    z = c_gate * xv
    L = z.shape[1]
    zp = jnp.pad(z, ((0, 0), (CONV_WIDTH - 1, 0), (0, 0)))
    conv = sum(conv_w[j] * zp[:, j:j + L] for j in range(CONV_WIDTH))
    return b_gate * conv


def s5_mixer(u, a_re, a_im, log_dt, b_re, b_im, c_re, c_im, d_skip, w_glu):
    Bsz, L, _ = u.shape
    ug = u.reshape(Bsz, L, S5_GROUPS, S5_CH).astype(F32)
    a = lax.complex(a_re.astype(F32), a_im.astype(F32))
    dt = jnp.exp(log_dt.astype(F32))[:, None]
    a_bar = jnp.exp(dt * a)
    b_bar = ((a_bar - 1.0) / a)[:, :, None] * lax.complex(b_re.astype(F32), b_im.astype(F32))
    bu = jnp.einsum('gpc,blgc->blgp', b_bar, ug)
    a_seq = jnp.broadcast_to(a_bar, (L,) + a_bar.shape)

    def combine(e1, e2):
        a1, b1 = e1
        a2, b2 = e2
        return a1 * a2, a2 * b1 + b2

    state = jax.vmap(lambda bu_b: lax.associative_scan(combine, (a_seq, bu_b), axis=0)[1])(bu)
    c = lax.complex(c_re.astype(F32), c_im.astype(F32))
    y = jnp.real(jnp.einsum('gcp,blgp->blgc', c, state)) + d_skip.astype(F32) * ug
    y = jax.nn.gelu(y)
    y = y * jax.nn.sigmoid(jnp.einsum('gcd,blgd->blgc', w_glu.astype(F32), y))
    return y.reshape(Bsz, L, D_SSM).astype(u.dtype)


def conv_ssm_mixer(h, w_in, conv_w, a_re, a_im, log_dt, b_re, b_im, c_re, c_im, d_skip, w_glu, w_out):
    p = h @ w_in
    b_gate, c_gate, xv, u = jnp.split(p, [D_CONV, 2 * D_CONV, 3 * D_CONV], axis=-1)
    y = jnp.concatenate([short_conv_mixer(b_gate, c_gate, xv, conv_w),
                         s5_mixer(u, a_re, a_im, log_dt, b_re, b_im, c_re, c_im, d_skip, w_glu)], axis=-1)
    return y @ w_out


def mlstm(q, k, v, o_pre, i_pre, f_pre, b_i, b_f):
    Bsz, L, _ = q.shape
    nc = L // CHUNK

    def to_chunks(t):
        return t.reshape(Bsz, nc, CHUNK, MLSTM_HEADS, MLSTM_DH).transpose(1, 0, 3, 2, 4).astype(F32)

    def gate_chunks(t):
        return t.reshape(Bsz, nc, CHUNK, MLSTM_HEADS).transpose(1, 0, 3, 2)

    qc, kc, vc = to_chunks(q), to_chunks(k) * MLSTM_DH ** -0.5, to_chunks(v)
    li = gate_chunks(i_pre.astype(F32) + b_i.astype(F32))
    lf = gate_chunks(jax.nn.log_sigmoid(f_pre.astype(F32) + b_f.astype(F32)))
    tril = jnp.tril(jnp.ones((CHUNK, CHUNK), bool))

    def step(carry, xs):
        c_st, n_st, m_prev = carry
        qx, kx, vx, lix, lfx = xs
        bcum = jnp.cumsum(lfx, axis=-1)
        dmat = jnp.where(tril, bcum[..., :, None] - bcum[..., None, :] + lix[..., None, :], -jnp.inf)
        m_inter = bcum + m_prev[..., None]
        m_t = jnp.maximum(m_inter, dmat.max(-1))
        s = jnp.einsum('bhtd,bhsd->bhts', qx, kx) * jnp.exp(dmat - m_t[..., None])
        w_inter = jnp.exp(m_inter - m_t)
        num = jnp.einsum('bhts,bhsd->bhtd', s, vx) + w_inter[..., None] * jnp.einsum('bhvk,bhtk->bhtv', c_st, qx)
        den = s.sum(-1) + w_inter * jnp.einsum('bhtk,bhk->bht', qx, n_st)
        h_out = num / jnp.maximum(jnp.abs(den), jnp.exp(-m_t))[..., None]
        b_last = bcum[..., -1]
        w_s = b_last[..., None] - bcum + lix
        m_new = jnp.maximum(b_last + m_prev, w_s.max(-1))
        decay = jnp.exp(b_last + m_prev - m_new)
        ws = jnp.exp(w_s - m_new[..., None])
        c_new = decay[..., None, None] * c_st + jnp.einsum('bhs,bhsv,bhsk->bhvk', ws, vx, kx)
        n_new = decay[..., None] * n_st + jnp.einsum('bhs,bhsk->bhk', ws, kx)
        return (c_new, n_new, m_new), h_out

    init = (jnp.zeros((Bsz, MLSTM_HEADS, MLSTM_DH, MLSTM_DH), F32),
            jnp.zeros((Bsz, MLSTM_HEADS, MLSTM_DH), F32),
            jnp.zeros((Bsz, MLSTM_HEADS), F32))
    _, hc = lax.scan(step, init, (qc, kc, vc, li, lf))
    h_all = hc.transpose(1, 0, 3, 2, 4).reshape(Bsz, L, D_MLSTM)
    return (jax.nn.sigmoid(o_pre.astype(F32)) * h_all).astype(q.dtype)


def dsa_attention(q, k, v, q_idx, k_idx, w_idx, rel_bias_table, topk):
    Bsz, L, _ = q.shape
    nb = L // Q_BLOCK
    pos = jnp.arange(L, dtype=jnp.int32)
    key_chunk = chunk_id(pos)
    qh = q.reshape(Bsz, L, DSA_HEADS, DSA_DH)
    kh = k.reshape(Bsz, L, DSA_HEADS, DSA_DH)
    vh = v.reshape(Bsz, L, DSA_HEADS, DSA_DH)
    qi = q_idx.reshape(Bsz, L, IDX_HEADS, IDX_DH)
    wi = w_idx * IDX_HEADS ** -0.5
    gather = jax.vmap(lambda t, i: t[i])

    def blocks(t):
        return jnp.moveaxis(t.reshape((Bsz, nb, Q_BLOCK) + t.shape[2:]), 1, 0)

    def attend(args):
        q_b, qi_b, w_b, pos_b = args
        s = jnp.einsum('bqhd,bkd->bqhk', qi_b, k_idx).astype(F32) * IDX_DH ** -0.5
        score = jnp.einsum('bqh,bqhk->bqk', w_b.astype(F32), jax.nn.relu(s))
        q_chunk = chunk_id(pos_b)
        admissible = key_chunk[None, :] <= q_chunk[:, None]
        score = jnp.where(admissible[None], score, -jnp.inf)
        _, idx = lax.top_k(score, topk)
        k_sel = gather(kh, idx)
        v_sel = gather(vh, idx)
        logits = jnp.einsum('bqhd,bqkhd->bhqk', q_b, k_sel).astype(F32) * DSA_DH ** -0.5
        bias = rel_bias_table[t5_bucket(idx - pos_b[None, :, None])]
        valid = key_chunk[idx] <= q_chunk[None, :, None]
        logits = jnp.where(valid[:, None], logits + jnp.moveaxis(bias, -1, 1).astype(F32), -jnp.inf)
        p = jax.nn.softmax(logits, axis=-1)
        return jnp.einsum('bhqk,bqkhd->bqhd', p.astype(v.dtype), v_sel)

    out = lax.map(attend, (blocks(qh), blocks(qi), blocks(wi), pos.reshape(nb, Q_BLOCK)))
    return jnp.moveaxis(out, 0, 1).reshape(Bsz, L, D_DSA)


def mlstm_dsa_mixer(h, w_in, b_i, b_f, rel_bias_table, w_out, topk):
    Bsz, L, _ = h.shape
    lq = L + (-L) % Q_BLOCK
    hp = jnp.pad(h, ((0, 0), (0, lq - L), (0, 0)))
    p = hp @ w_in
    sizes = [D_MLSTM] * 4 + [MLSTM_HEADS] * 2 + [D_DSA] * 3 + [IDX_HEADS * IDX_DH, IDX_DH, IDX_HEADS]
    q_m, k_m, v_m, o_m, i_m, f_m, q_d, k_d, v_d, q_i, k_i, w_i = jnp.split(p, np.cumsum(sizes)[:-1].tolist(), axis=-1)
    y_m = mlstm(q_m, k_m, v_m, o_m, i_m, f_m, b_i, b_f)
    y_d = dsa_attention(q_d, k_d, v_d, q_i, k_i, w_i, rel_bias_table, topk)
    y = jnp.concatenate([y_m, y_d], axis=-1)[:, :L]
    return y @ w_out


def swiglu(h, w1, w3, w2):
    return (jax.nn.silu(h @ w1) * (h @ w3)) @ w2


def moe_swiglu(h, w_router, b_router, w1, w3, w2):
    Bsz, L, D = h.shape
    x2 = h.reshape(-1, D)
    n = x2.shape[0]
    logits = (x2 @ w_router).astype(F32) + b_router.astype(F32)
    top_val, top_idx = lax.top_k(logits, TOP_K)
    gates = jax.nn.softmax(top_val, axis=-1)
    na = n * TOP_K
    flat_e = top_idx.reshape(-1)
    order = jnp.argsort(flat_e)
    e_s = flat_e[order]
    tok_s = (order // TOP_K).astype(jnp.int32)
    g_s = gates.reshape(-1)[order]
    counts = jnp.bincount(flat_e, length=N_EXPERTS)
    padded = (counts + MOE_BLOCK - 1) // MOE_BLOCK * MOE_BLOCK
    pad_end = jnp.cumsum(padded)
    pad_start = pad_end - padded
    start = jnp.cumsum(counts) - counts
    dest = pad_start[e_s] + jnp.arange(na) - start[e_s]
    nblk = -(-na // MOE_BLOCK) + N_EXPERTS
    cap = nblk * MOE_BLOCK
    slot_tok = jnp.zeros((cap,), jnp.int32).at[dest].set(tok_s)
    slot_gate = jnp.zeros((cap,), F32).at[dest].set(g_s)
    blk_expert = jnp.minimum(jnp.searchsorted(pad_end, jnp.arange(nblk) * MOE_BLOCK, side='right'), N_EXPERTS - 1)
    xb = x2[slot_tok].reshape(nblk, MOE_BLOCK, D)

    def expert_block(args):
        xe, e = args
        return (jax.nn.silu(xe @ w1[e]) * (xe @ w3[e])) @ w2[e]

    yb = lax.map(expert_block, (xb, blk_expert)).reshape(cap, D)
    y = jax.ops.segment_sum(yb.astype(F32) * slot_gate[:, None], slot_tok, num_segments=n)
    return y.astype(h.dtype).reshape(Bsz, L, D)


def setup_inputs(seed: int = 0) -> dict:
    key = jax.random.key(seed)
    ks = iter(jax.random.split(key, 48))

    def nrm(shape, scale):
        return jax.random.normal(next(ks), shape, F32) * scale

    def gain(shape):
        return 1.0 + nrm(shape, 0.02)

    d = D_MODEL
    return {
        "x": nrm((BATCH, SEQ, d), 1.0),
        "meta_tokens": nrm((N_META, d), 1.0),
        "w_in_even": nrm((N_EVEN, d, D_IN_EVEN), d ** -0.5),
        "conv_w": nrm((N_EVEN, CONV_WIDTH, D_CONV), CONV_WIDTH ** -0.5),
        "s5_a_re": -0.5 + nrm((N_EVEN, S5_GROUPS, S5_STATE), 0.01),
        "s5_a_im": math.pi * jnp.arange(S5_STATE, dtype=F32) + nrm((N_EVEN, S5_GROUPS, S5_STATE), 0.01),
        "s5_log_dt": jax.random.uniform(next(ks), (N_EVEN, S5_GROUPS), F32, math.log(1e-3), math.log(1e-1)),
        "s5_b_re": nrm((N_EVEN, S5_GROUPS, S5_STATE, S5_CH), (2 * S5_CH) ** -0.5),
        "s5_b_im": nrm((N_EVEN, S5_GROUPS, S5_STATE, S5_CH), (2 * S5_CH) ** -0.5),
        "s5_c_re": nrm((N_EVEN, S5_GROUPS, S5_CH, S5_STATE), S5_STATE ** -0.5),
        "s5_c_im": nrm((N_EVEN, S5_GROUPS, S5_CH, S5_STATE), S5_STATE ** -0.5),
        "s5_d": nrm((N_EVEN, S5_GROUPS, S5_CH), 1.0),
        "s5_w_glu": nrm((N_EVEN, S5_GROUPS, S5_CH, S5_CH), S5_CH ** -0.5),
        "w_out_even": nrm((N_EVEN, D_CONV + D_SSM, d), BETA * (D_CONV + D_SSM) ** -0.5),
        "ln_mix_even_g": gain((N_EVEN, d)),
        "ln_mix_even_b": nrm((N_EVEN, d), 0.02),
        "ffn_w1": nrm((N_EVEN, d, D_FF), d ** -0.5),
        "ffn_w3": nrm((N_EVEN, d, D_FF), d ** -0.5),
        "ffn_w2": nrm((N_EVEN, D_FF, d), BETA * D_FF ** -0.5),
        "ln_ffn_even_g": gain((N_EVEN, d)),
        "ln_ffn_even_b": nrm((N_EVEN, d), 0.02),
        "w_in_odd": nrm((N_ODD, d, D_IN_ODD), d ** -0.5),
        "mlstm_b_i": nrm((N_ODD, MLSTM_HEADS), 0.1),
        "mlstm_b_f": jnp.linspace(3.0, 6.0, MLSTM_HEADS, dtype=F32) + nrm((N_ODD, MLSTM_HEADS), 0.1),
        "rel_bias_table": nrm((NUM_BUCKETS, DSA_HEADS), 0.2),
        "w_out_odd": nrm((N_ODD, D_MLSTM + D_DSA, d), BETA * (D_MLSTM + D_DSA) ** -0.5),
        "ln_mix_odd_g": gain((N_ODD, d)),
        "ln_mix_odd_b": nrm((N_ODD, d), 0.02),
        "router_w": nrm((N_ODD, d, N_EXPERTS), d ** -0.5),
        "router_b": nrm((N_ODD, N_EXPERTS), 0.01),
        "expert_w1": nrm((N_ODD, N_EXPERTS, d, D_EXPERT), d ** -0.5),
        "expert_w3": nrm((N_ODD, N_EXPERTS, d, D_EXPERT), d ** -0.5),
        "expert_w2": nrm((N_ODD, N_EXPERTS, D_EXPERT, d), BETA * D_EXPERT ** -0.5),
        "ln_ffn_odd_g": gain((N_ODD, d)),
        "ln_ffn_odd_b": nrm((N_ODD, d), 0.02),
    }


def reference(x, meta_tokens, w_in_even, conv_w, s5_a_re, s5_a_im, s5_log_dt, s5_b_re, s5_b_im, s5_c_re, s5_c_im,
              s5_d, s5_w_glu, w_out_even, ln_mix_even_g, ln_mix_even_b, ffn_w1, ffn_w3, ffn_w2, ln_ffn_even_g,
              ln_ffn_even_b, w_in_odd, mlstm_b_i, mlstm_b_f, rel_bias_table, w_out_odd, ln_mix_odd_g, ln_mix_odd_b,
              router_w, router_b, expert_w1, expert_w3, expert_w2, ln_ffn_odd_g, ln_ffn_odd_b):
    Bsz, S, D = x.shape
    topk = min(TOPK_MAX, S // 4)
    h = jnp.concatenate([jnp.broadcast_to(meta_tokens.astype(x.dtype)[None], (Bsz, N_META, D)), x], axis=1)
    for layer in range(DEPTH):
        i = layer // 2
        if layer % 2 == 0:
            mix = conv_ssm_mixer(h, w_in_even[i], conv_w[i], s5_a_re[i], s5_a_im[i], s5_log_dt[i], s5_b_re[i],
                                 s5_b_im[i], s5_c_re[i], s5_c_im[i], s5_d[i], s5_w_glu[i], w_out_even[i])
            h = layer_norm(ALPHA * h + mix, ln_mix_even_g[i], ln_mix_even_b[i])
            h = layer_norm(ALPHA * h + swiglu(h, ffn_w1[i], ffn_w3[i], ffn_w2[i]), ln_ffn_even_g[i], ln_ffn_even_b[i])
        else:
            mix = mlstm_dsa_mixer(h, w_in_odd[i], mlstm_b_i[i], mlstm_b_f[i], rel_bias_table, w_out_odd[i], topk)
            h = layer_norm(ALPHA * h + mix, ln_mix_odd_g[i], ln_mix_odd_b[i])
            moe = moe_swiglu(h, router_w[i], router_b[i], expert_w1[i], expert_w3[i], expert_w2[i])
            h = layer_norm(ALPHA * h + moe, ln_ffn_odd_g[i], ln_ffn_odd_b[i])
    return h[:, N_META:]
```

```python
import functools
import math

import jax
import jax.numpy as jnp
import numpy as np
from jax import lax
from jax.experimental import pallas as pl
from jax.experimental.pallas import tpu as pltpu

F32 = jnp.float32
BF16 = jnp.bfloat16
I32 = jnp.int32

D_MODEL = 2048
DEPTH = 2
CHUNK = 64
N_META = 16
CONV_WIDTH = 3
D_CONV = 1536
S5_CH = 16
S5_STATE = 64
D_SSM = 512
S5_GROUPS = D_SSM // S5_CH
MLSTM_HEADS = 4
MLSTM_DH = 256
D_MLSTM = MLSTM_HEADS * MLSTM_DH
DSA_HEADS = 8
DSA_DH = 128
D_DSA = DSA_HEADS * DSA_DH
IDX_HEADS = 8
IDX_DH = 64
TOPK_MAX = 256
NUM_BUCKETS = 32
MAX_DISTANCE = 128
D_FF = 5632
N_EXPERTS = 8
TOP_K = 2
D_EXPERT = 5632
ALPHA = (2 * DEPTH) ** 0.25
LN_EPS = 1e-5

LANES = 128
SEQ_TILE = 256
S5_T = 64
MOE_BLK = 512
VMEM_LIMIT = 56 * 1024 * 1024
NEG = -1e30
INT_MIN = -(2 ** 31)


def _params(sem, vmem=VMEM_LIMIT):
    return pltpu.CompilerParams(dimension_semantics=sem, vmem_limit_bytes=vmem)


def _pick(n, candidates):
    for c in candidates:
        if n % c == 0:
            return c
    raise ValueError(f"no tile in {candidates} divides {n}")


def _dot(a, b):
    return jnp.dot(a, b, preferred_element_type=F32)


def _dot_nt(a, b):
    return lax.dot_general(a, b, (((1,), (1,)), ((), ())), preferred_element_type=F32)


def _split3(x):
    hi = x.astype(BF16)
    r = x - hi.astype(F32)
    mid = r.astype(BF16)
    lo = (r - mid.astype(F32)).astype(BF16)
    return hi, mid, lo


def _layer_norm(z, g, b):
    mu = jnp.mean(z, axis=-1, keepdims=True)
    zc = z - mu
    var = jnp.mean(zc * zc, axis=-1, keepdims=True)
    return zc * lax.rsqrt(var + LN_EPS) * g + b


def _proj_kernel(x_ref, w_ref, o_ref):
    o_ref[...] = _dot(x_ref[...], w_ref[...]).astype(o_ref.dtype)


def _proj(x, w, out_dtype, tm, tn):
    n, k = x.shape
    m = w.shape[1]
    return pl.pallas_call(
        _proj_kernel,
        grid=(n // tm, m // tn),
        in_specs=[pl.BlockSpec((tm, k), lambda i, j: (i, 0)),
                  pl.BlockSpec((k, tn), lambda i, j: (0, j))],
        out_specs=pl.BlockSpec((tm, tn), lambda i, j: (i, j)),
        out_shape=jax.ShapeDtypeStruct((n, m), out_dtype),
        compiler_params=_params(("parallel", "parallel")),
        name="proj",
    )(x, w)


def _outproj_ln_kernel(ya_ref, yb_ref, wa_ref, wb_ref, h_ref, g_ref, b_ref, o_ref, ob_ref):
    mix = _dot(ya_ref[...], wa_ref[...]) + _dot(yb_ref[...], wb_ref[...])
    out = _layer_norm(ALPHA * h_ref[...] + mix, g_ref[...], b_ref[...])
    o_ref[...] = out
    ob_ref[...] = out.astype(BF16)


def _outproj_ln(ya, yb, w, h, g, b, tm):
    n, d = h.shape
    ka, kb = ya.shape[1], yb.shape[1]
    wa, wb = w[:ka].astype(BF16), w[ka:].astype(BF16)
    row = lambda i: (i, 0)
    fixed = lambda i: (0, 0)
    return pl.pallas_call(
        _outproj_ln_kernel,
        grid=(n // tm,),
        in_specs=[pl.BlockSpec((tm, ka), row), pl.BlockSpec((tm, kb), row),
                  pl.BlockSpec((ka, d), fixed), pl.BlockSpec((kb, d), fixed),
                  pl.BlockSpec((tm, d), row),
                  pl.BlockSpec((1, d), fixed), pl.BlockSpec((1, d), fixed)],
        out_specs=[pl.BlockSpec((tm, d), row), pl.BlockSpec((tm, d), row)],
        out_shape=[jax.ShapeDtypeStruct((n, d), F32), jax.ShapeDtypeStruct((n, d), BF16)],
        compiler_params=_params(("parallel",)),
        name="outproj_ln",
    )(ya, yb, wa, wb, h, g.reshape(1, d), b.reshape(1, d))


def _conv_kernel(b_ref, c_ref, x_ref, ch_ref, xh_ref, w_ref, y_ref, *, tiles_per_seq):
    i = pl.program_id(0)
    z = c_ref[...] * x_ref[...]
    zh = ch_ref[...] * xh_ref[...]
    zh = jnp.where(i % tiles_per_seq == 0, 0.0, zh)
    r = lax.broadcasted_iota(I32, z.shape, 0)
    z1 = jnp.where(r == 0, zh[7:8], pltpu.roll(z, 1, 0))
    z2 = jnp.where(r == 0, zh[6:7], jnp.where(r == 1, zh[7:8], pltpu.roll(z, 2, 0)))
    w = w_ref[...]
    conv = w[0:1] * z2 + w[1:2] * z1 + w[2:3] * z
    y_ref[...] = (b_ref[...] * conv).astype(y_ref.dtype)


def _conv_mixer(p, conv_w, lp, tm, tc):
    n = p.shape[0]
    ncb = D_CONV // tc
    hb = tm // 8
    body = functools.partial(_conv_kernel, tiles_per_seq=lp // tm)
    return pl.pallas_call(
        body,
        grid=(n // tm, ncb),
        in_specs=[pl.BlockSpec((tm, tc), lambda i, j: (i, j)),
                  pl.BlockSpec((tm, tc), lambda i, j: (i, ncb + j)),
                  pl.BlockSpec((tm, tc), lambda i, j: (i, 2 * ncb + j)),
                  pl.BlockSpec((8, tc), lambda i, j: (jnp.maximum(i * hb - 1, 0), ncb + j)),
                  pl.BlockSpec((8, tc), lambda i, j: (jnp.maximum(i * hb - 1, 0), 2 * ncb + j)),
                  pl.BlockSpec((CONV_WIDTH, tc), lambda i, j: (0, j))],
        out_specs=pl.BlockSpec((tm, tc), lambda i, j: (i, j)),
        out_shape=jax.ShapeDtypeStruct((n, D_CONV), BF16),
        compiler_params=_params(("parallel", "parallel")),
        name="conv_mixer",
    )(p, p, p, p, p, conv_w)


def _s5_operators(a_re, a_im, log_dt, b_re, b_im, c_re, c_im, t):
    a = lax.complex(a_re.astype(F32), a_im.astype(F32))
    dt = jnp.exp(log_dt.astype(F32))[:, None]
    a_bar = jnp.exp(dt * a)
    b_bar = ((a_bar - 1.0) / a)[:, :, None] * lax.complex(b_re.astype(F32), b_im.astype(F32))
    c = lax.complex(c_re.astype(F32), c_im.astype(F32))
    lag = jnp.arange(t + 1, dtype=F32)[:, None, None]
    pw = jnp.exp(lag * (dt * a)[None])
    kern = jnp.real(jnp.einsum('gcp,lgp,gpd->glcd', c, pw[:t], b_bar))
    i_idx = jnp.arange(t)[None, :]
    j_idx = jnp.arange(t)[:, None]
    lagm = i_idx - j_idx
    blocks = jnp.where((lagm >= 0)[None, :, :, None, None], kern[:, jnp.maximum(lagm, 0)], 0.0)
    g = a.shape[0]
    toep = blocks.transpose(0, 1, 4, 2, 3).reshape(g, t * S5_CH, t * S5_CH)
    wc = pw[:t][::-1].transpose(1, 0, 2)[:, :, None, :] * b_bar.transpose(0, 2, 1)[:, None, :, :]
    wc = wc.reshape(g, t * S5_CH, S5_STATE)
    w_in = jnp.concatenate([jnp.real(wc), jnp.imag(wc)], axis=-1)
    cp = c.transpose(0, 2, 1)[:, :, None, :] * pw[1:t + 1].transpose(1, 2, 0)[:, :, :, None]
    cp = cp.reshape(g, S5_STATE, t * S5_CH)
    v_out = jnp.concatenate([jnp.real(cp), -jnp.imag(cp)], axis=1)
    a_t = jnp.stack([jnp.real(pw[t]), jnp.imag(pw[t])], axis=1)
    return toep.astype(BF16), w_in.astype(BF16), v_out.astype(BF16), a_t


def _s5_kernel(u_ref, toep_ref, win_ref, vout_ref, at_ref, y_ref, er_ref, ei_ref, hr_ref, hi_ref, *, nbatch, nc):
    u = u_ref[...]
    e = _dot(u, win_ref[...])
    er_ref[...] = e[:, :S5_STATE]
    ei_ref[...] = e[:, S5_STATE:]
    at = at_ref[...]
    ar, ai = at[0:1], at[1:2]

    def step(c, carry):
        out = []
        for b in range(nbatch):
            hr, hi = carry[2 * b], carry[2 * b + 1]
            row = b * nc + c
            hr_ref[pl.ds(row, 1), :] = hr
            hi_ref[pl.ds(row, 1), :] = hi
            er = er_ref[pl.ds(row, 1), :]
            ei = ei_ref[pl.ds(row, 1), :]
            out += [ar * hr - ai * hi + er, ar * hi + ai * hr + ei]
        return tuple(out)

    zero = jnp.zeros((1, S5_STATE), F32)
    lax.fori_loop(0, nc, step, (zero,) * (2 * nbatch))
    vout = vout_ref[...]
    y = _dot(u, toep_ref[...])
    for part, rows in ((hr_ref[...], vout[:S5_STATE]), (hi_ref[...], vout[S5_STATE:])):
        hi_, mid_, _ = _split3(part)
        y += _dot(hi_, rows) + _dot(mid_, rows)
    y_ref[...] = y


def _s5_linear(u, ops, nbatch, lp):
    toep, w_in, v_out, a_t = ops
    n = u.shape[0]
    t = S5_T
    nc = lp // t
    rows = nbatch * nc
    width = t * S5_CH
    ug = u.reshape(rows, t, S5_GROUPS, S5_CH).transpose(2, 0, 1, 3).reshape(S5_GROUPS, rows, width)
    grp = lambda g: (g, 0, 0)
    yg = pl.pallas_call(
        functools.partial(_s5_kernel, nbatch=nbatch, nc=nc),
        grid=(S5_GROUPS,),
        in_specs=[pl.BlockSpec((None, rows, width), grp),
                  pl.BlockSpec((None, width, width), grp),
                  pl.BlockSpec((None, width, 2 * S5_STATE), grp),
                  pl.BlockSpec((None, 2 * S5_STATE, width), grp),
                  pl.BlockSpec((None, 2, S5_STATE), grp)],
        out_specs=pl.BlockSpec((None, rows, width), grp),
        out_shape=jax.ShapeDtypeStruct((S5_GROUPS, rows, width), F32),
        scratch_shapes=[pltpu.VMEM((rows, S5_STATE), F32)] * 4,
        compiler_params=_params(("parallel",)),
        name="s5_linear",
    )(ug, toep, w_in, v_out, a_t)
    return yg.reshape(S5_GROUPS, rows, t, S5_CH).transpose(1, 2, 0, 3).reshape(n, D_SSM)


def _s5_glu_kernel(y_ref, u_ref, d_ref, w_ref, o_ref):
    y = y_ref[...] + d_ref[...] * u_ref[...]
    y = jax.nn.gelu(y)
    hi, mid, _ = _split3(y)
    w = w_ref[...]
    gate = _dot(hi, w) + _dot(mid, w)
    o_ref[...] = (y * jax.nn.sigmoid(gate)).astype(o_ref.dtype)


def _s5_glu(y_lin, p, d_skip, w_glu, tm):
    n = y_lin.shape[0]
    ucol = (3 * D_CONV) // D_SSM
    wbd = jnp.einsum('gcd,gh->gdhc', w_glu.astype(F32), jnp.eye(S5_GROUPS, dtype=F32)).reshape(D_SSM, D_SSM)
    return pl.pallas_call(
        _s5_glu_kernel,
        grid=(n // tm,),
        in_specs=[pl.BlockSpec((tm, D_SSM), lambda i: (i, 0)),
                  pl.BlockSpec((tm, D_SSM), lambda i: (i, ucol)),
                  pl.BlockSpec((1, D_SSM), lambda i: (0, 0)),
                  pl.BlockSpec((D_SSM, D_SSM), lambda i: (0, 0))],
        out_specs=pl.BlockSpec((tm, D_SSM), lambda i: (i, 0)),
        out_shape=jax.ShapeDtypeStruct((n, D_SSM), BF16),
        compiler_params=_params(("parallel",)),
        name="s5_glu",
    )(y_lin, p, d_skip.reshape(1, D_SSM).astype(F32), wbd.astype(BF16))


def _ffn_ln_kernel(h_ref, w1_ref, w3_ref, w2_ref, g_ref, b_ref, o_ref, ob_ref, xb_ref, acc_ref, *, nf):
    f = pl.program_id(1)

    @pl.when(f == 0)
    def _():
        xb_ref[...] = h_ref[...].astype(BF16)
        acc_ref[...] = jnp.zeros_like(acc_ref)

    x = xb_ref[...]
    a = _dot(x, w1_ref[...])
    c = _dot(x, w3_ref[...])
    mid = (a * jax.nn.sigmoid(a) * c).astype(BF16)
    acc_ref[...] += _dot(mid, w2_ref[...])

    @pl.when(f == nf - 1)
    def _():
        out = _layer_norm(ALPHA * h_ref[...] + acc_ref[...], g_ref[...], b_ref[...])
        o_ref[...] = out
        ob_ref[...] = out.astype(BF16)


def _ffn_ln(h, w1, w3, w2, g, b, tm, tf):
    n, d = h.shape
    nf = w1.shape[1] // tf
    return pl.pallas_call(
        functools.partial(_ffn_ln_kernel, nf=nf),
        grid=(n // tm, nf),
        in_specs=[pl.BlockSpec((tm, d), lambda i, f: (i, 0)),
                  pl.BlockSpec((d, tf), lambda i, f: (0, f)),
                  pl.BlockSpec((d, tf), lambda i, f: (0, f)),
                  pl.BlockSpec((tf, d), lambda i, f: (f, 0)),
                  pl.BlockSpec((1, d), lambda i, f: (0, 0)),
                  pl.BlockSpec((1, d), lambda i, f: (0, 0))],
        out_specs=[pl.BlockSpec((tm, d), lambda i, f: (i, 0)), pl.BlockSpec((tm, d), lambda i, f: (i, 0))],
        out_shape=[jax.ShapeDtypeStruct((n, d), F32), jax.ShapeDtypeStruct((n, d), BF16)],
        scratch_shapes=[pltpu.VMEM((tm, d), BF16), pltpu.VMEM((tm, d), F32)],
        compiler_params=_params(("parallel", "arbitrary")),
        name="ffn_ln",
    )(h, w1.astype(BF16), w3.astype(BF16), w2.astype(BF16), g.reshape(1, d), b.reshape(1, d))


def _log_sigmoid(x):
    return jnp.minimum(x, 0.0) - jnp.log(1.0 + jnp.exp(-jnp.abs(x)))


def _mlstm_kernel(q_ref, k_ref, v_ref, o_ref, g_ref, gb_ref, y_ref, ct_ref, n_ref, m_ref):
    c = pl.program_id(1)
    t = q_ref.shape[0]
    dh = MLSTM_DH

    @pl.when(c == 0)
    def _():
        ct_ref[...] = jnp.zeros_like(ct_ref)
        n_ref[...] = jnp.zeros_like(n_ref)
        m_ref[...] = jnp.zeros_like(m_ref)

    gates = g_ref[...] + gb_ref[...]
    lf = _log_sigmoid(gates)
    row = lax.broadcasted_iota(I32, (t, t), 0)
    col = lax.broadcasted_iota(I32, (t, t), 1)
    tril = col <= row
    tril_b = jnp.where(tril, 1.0, 0.0).astype(BF16)
    hi, mid, lo = _split3(lf)
    bcum = _dot(tril_b, hi) + _dot(tril_b, mid) + _dot(tril_b, lo)
    gates_t = gates.T
    bcum_t = bcum.T
    for h in range(MLSTM_HEADS):
        sl = slice(h * dh, (h + 1) * dh)
        fh = MLSTM_HEADS + h
        li_row, li_col = gates_t[h:h + 1, :], gates[:, h:h + 1]
        bc_row, bc_col = bcum_t[fh:fh + 1, :], bcum[:, fh:fh + 1]
        m_prev = m_ref[h:h + 1, 0:1]
        dmat = jnp.where(tril, bc_col - bc_row + li_row, -jnp.inf)
        m_inter = bc_col + m_prev
        m_t = jnp.maximum(m_inter, jnp.max(dmat, axis=1, keepdims=True))
        q = q_ref[:, sl]
        k = k_ref[:, sl] * (dh ** -0.5)
        v = v_ref[:, sl]
        s = _dot_nt(q, k) * jnp.exp(dmat - m_t)
        w_inter = jnp.exp(m_inter - m_t)
        ct = ct_ref[h]
        nvec = n_ref[h:h + 1, :]
        num = _dot(s.astype(BF16), v) + w_inter * _dot(q, ct.astype(BF16))
        den = jnp.sum(s, axis=1, keepdims=True) + w_inter * jnp.sum(q.astype(F32) * nvec, axis=1, keepdims=True)
        h_out = num / jnp.maximum(jnp.abs(den), jnp.exp(-m_t))
        y_ref[:, sl] = (jax.nn.sigmoid(o_ref[:, sl].astype(F32)) * h_out).astype(y_ref.dtype)
        b_last = bc_col[t - 1:t, :]
        m_new = jnp.maximum(b_last + m_prev, jnp.max(b_last - bc_row + li_row, axis=1, keepdims=True))
        decay = jnp.exp(b_last + m_prev - m_new)
        kw = k.astype(F32) * jnp.exp(b_last - bc_col + li_col - m_new)
        ct_ref[h] = decay * ct + _dot(kw.T.astype(BF16), v)
        n_ref[h:h + 1, :] = decay * nvec + jnp.sum(kw, axis=0, keepdims=True)
        m_ref[h:h + 1, :] = jnp.broadcast_to(m_new, (1, LANES))


def _mlstm(big, small, gate_bias, nbatch, lp):
    n = big.shape[0]
    t = SEQ_TILE
    nc = lp // t
    blk = lambda j: pl.BlockSpec((t, D_MLSTM), lambda b, c: (b * nc + c, j))
    return pl.pallas_call(
        _mlstm_kernel,
        grid=(nbatch, nc),
        in_specs=[blk(0), blk(1), blk(2), blk(3),
                  pl.BlockSpec((t, LANES), lambda b, c: (b * nc + c, 0)),
                  pl.BlockSpec((1, LANES), lambda b, c: (0, 0))],
        out_specs=pl.BlockSpec((t, D_MLSTM), lambda b, c: (b * nc + c, 0)),
        out_shape=jax.ShapeDtypeStruct((n, D_MLSTM), BF16),
        scratch_shapes=[pltpu.VMEM((MLSTM_HEADS, MLSTM_DH, MLSTM_DH), F32),
                        pltpu.VMEM((8, MLSTM_DH), F32),
                        pltpu.VMEM((8, LANES), F32)],
        compiler_params=_params(("parallel", "arbitrary")),
        name="mlstm",
    )(big, big, big, big, small, gate_bias)


def _chunk_end(pos):
    return jnp.where(pos < N_META, N_META, N_META + CHUNK + CHUNK * ((pos - N_META) >> 6))


def _t5_bucket(rel):
    half = NUM_BUCKETS // 2
    max_exact = half // 2
    ret = jnp.where(rel > 0, half, 0)
    n = jnp.abs(rel)
    nf = jnp.maximum(n, 1).astype(F32)
    large = max_exact + (jnp.log(nf / max_exact) / math.log(MAX_DISTANCE / max_exact) * (half - max_exact)).astype(I32)
    large = jnp.minimum(large, half - 1)
    return ret + jnp.where(n < max_exact, n, large)


def _index_keys(qi_ref, kidx, wi):
    score = None
    for h in range(IDX_HEADS):
        s = _dot_nt(qi_ref[h], kidx) * (IDX_DH ** -0.5)
        term = wi[:, h:h + 1] * jnp.maximum(s, 0.0)
        score = term if score is None else score + term
    bits = lax.bitcast_convert_type(score, I32)
    return bits ^ ((bits >> 31) & 0x7FFFFFFF)


def _dsa_select_kernel(qi_ref, kidx_ref, w_ref, thr_ref, lim_ref, keys_ref, *, lp, topk):
    qb = SEQ_TILE
    b = pl.program_id(1)
    nkt = jnp.minimum(b + 2, lp // qb)
    wi = w_ref[...] * (IDX_HEADS ** -0.5)
    end = _chunk_end(b * qb + lax.broadcasted_iota(I32, (qb, 1), 0))
    lane = lax.broadcasted_iota(I32, (qb, qb), 1)

    def fill(kt, carry):
        off = pl.multiple_of(kt * qb, qb)
        keys = _index_keys(qi_ref, kidx_ref[pl.ds(off, qb), :], wi)
        keys_ref[:, pl.ds(off, qb)] = jnp.where(off + lane < end, keys, INT_MIN)
        return carry

    lax.fori_loop(0, nkt, fill, 0)

    def count(pred):
        def body(kt, acc):
            off = pl.multiple_of(kt * qb, qb)
            hit = jnp.where(pred(keys_ref[:, pl.ds(off, qb)], off + lane), 1, 0)
            for j in range(qb // LANES):
                acc = acc + hit[:, j * LANES:(j + 1) * LANES]
            return acc
        acc = lax.fori_loop(0, nkt, body, jnp.zeros((qb, LANES), I32))
        return jnp.sum(acc, axis=1, keepdims=True)

    c0 = count(lambda kk, _: kk >= 0)
    thr0 = jnp.where(c0 >= topk, 0, INT_MIN)

    def bit_round(i, thr):
        cand = thr | jnp.left_shift(jnp.int32(1), 30 - i)
        cnt = count(lambda kk, _: kk >= cand)
        return jnp.where(cnt >= topk, cand, thr)

    thr = lax.fori_loop(0, 31, bit_round, thr0)
    thr_ref[...] = thr
    need = topk - count(lambda kk, _: kk > thr)
    tie = (count(lambda kk, _: kk >= thr) > topk) & (thr != INT_MIN)
    lim_ref[...] = jnp.full((qb, 1), lp, I32)

    @pl.when(jnp.max(jnp.where(tie, 1, 0)) > 0)
    def _():
        def idx_round(i, x):
            cand = x | jnp.left_shift(jnp.int32(1), 14 - i)
            cnt = count(lambda kk, cc: (kk == thr) & (cc < cand))
            return jnp.where(cnt < need, cand, x)
        x = lax.fori_loop(0, 15, idx_round, jnp.zeros((qb, 1), I32))
        lim_ref[...] = jnp.where(tie, x, lp)


def _dsa_select(qi, kidx, w, nbatch, lp, topk):
    qb = SEQ_TILE
    nqb = lp // qb
    return pl.pallas_call(
        functools.partial(_dsa_select_kernel, lp=lp, topk=topk),
        grid=(nbatch, nqb),
        in_specs=[pl.BlockSpec((None, IDX_HEADS, qb, IDX_DH), lambda bb, b: (bb, 0, b, 0)),
                  pl.BlockSpec((None, lp, IDX_DH), lambda bb, b: (bb, 0, 0)),
                  pl.BlockSpec((qb, IDX_HEADS), lambda bb, b: (bb * nqb + b, 0))],
        out_specs=[pl.BlockSpec((qb, 1), lambda bb, b: (bb * nqb + b, 0)),
                   pl.BlockSpec((qb, 1), lambda bb, b: (bb * nqb + b, 0))],
        out_shape=[jax.ShapeDtypeStruct((nbatch * lp, 1), I32), jax.ShapeDtypeStruct((nbatch * lp, 1), I32)],
        scratch_shapes=[pltpu.VMEM((qb, lp), I32)],
        compiler_params=_params(("parallel", "parallel")),
        name="dsa_select",
    )(qi, kidx, w)


def _dsa_attn_kernel(qb_ref, kt_ref, far_ref, q_ref, k_ref, v_ref, qi_ref, kidx_ref, w_ref, thr_ref, lim_ref,
                     band_ref, y_ref, m_ref, l_ref, acc_ref, *, lp):
    t = SEQ_TILE
    step = pl.program_id(1)
    b = qb_ref[step]
    kt = kt_ref[step]
    last_kt = jnp.minimum(b + 1, lp // t - 1)

    @pl.when(kt == 0)
    def _():
        m_ref[...] = jnp.full_like(m_ref, NEG)
        l_ref[...] = jnp.zeros_like(l_ref)
        acc_ref[...] = jnp.zeros_like(acc_ref)

    keys = _index_keys(qi_ref, kidx_ref[...], w_ref[...] * (IDX_HEADS ** -0.5))
    thr = thr_ref[...]
    colpos = kt * t + lax.broadcasted_iota(I32, (t, t), 1)
    sel = (keys > thr) | ((keys == thr) & (colpos <= lim_ref[...]))

    def heads(bias):
        for h in range(DSA_HEADS):
            sl = slice(h * DSA_DH, (h + 1) * DSA_DH)
            logits = _dot_nt(q_ref[:, sl], k_ref[:, sl]) * (DSA_DH ** -0.5)
            z = jnp.where(sel, logits + bias(h), NEG)
            m_old = m_ref[h]
            m_new = jnp.maximum(m_old, jnp.max(z, axis=1, keepdims=True))
            scale = jnp.exp(m_old - m_new)
            p = jnp.exp(z - m_new)
            l_ref[h] = scale * l_ref[h] + jnp.sum(p, axis=1, keepdims=True)
            acc_ref[:, sl] = scale * acc_ref[:, sl] + _dot(p.astype(BF16), v_ref[:, sl])
            m_ref[h] = m_new

    near = kt >= b - 1

    @pl.when(near)
    def _():
        heads(lambda h: band_ref[kt - b + 1, h])

    @pl.when(jnp.logical_not(near))
    def _():
        heads(lambda h: far_ref[h])

    @pl.when(kt == last_kt)
    def _():
        for h in range(DSA_HEADS):
            sl = slice(h * DSA_DH, (h + 1) * DSA_DH)
            y_ref[:, sl] = (acc_ref[:, sl] / l_ref[h]).astype(y_ref.dtype)


def _dsa_bias_band(rel_bias_table):
    t = SEQ_TILE
    i = jnp.arange(t, dtype=I32)[:, None]
    c = jnp.arange(3 * t, dtype=I32)[None, :] - t
    bias = rel_bias_table.astype(F32)[_t5_bucket(c - i)]
    band = jnp.where((c < _chunk_end(i))[:, :, None], bias, NEG)
    band = band.reshape(t, 3, t, DSA_HEADS).transpose(1, 3, 0, 2)
    far = rel_bias_table.astype(F32)[_t5_bucket(jnp.int32(-(t + 1)))]
    return band, far


def _dsa_attention(big, qi, kidx, w, thr, lim, rel_bias_table, nbatch, lp):
    t = SEQ_TILE
    nqb = lp // t
    n = big.shape[0]
    pairs = [(b, kt) for b in range(nqb) for kt in range(min(b + 2, nqb))]
    qb_of = jnp.asarray(np.array([p[0] for p in pairs], np.int32))
    kt_of = jnp.asarray(np.array([p[1] for p in pairs], np.int32))
    band, far = _dsa_bias_band(rel_bias_table)
    qcol, kcol, vcol = 4, 5, 6
    qrow = lambda bb, s, qr, kr: (bb * nqb + qr[s], 0)
    grid_spec = pltpu.PrefetchScalarGridSpec(
        num_scalar_prefetch=2,
        grid=(nbatch, len(pairs)),
        in_specs=[pl.BlockSpec(memory_space=pltpu.SMEM),
                  pl.BlockSpec((t, D_DSA), lambda bb, s, qr, kr: (bb * nqb + qr[s], qcol)),
                  pl.BlockSpec((t, D_DSA), lambda bb, s, qr, kr: (bb * nqb + kr[s], kcol)),
                  pl.BlockSpec((t, D_DSA), lambda bb, s, qr, kr: (bb * nqb + kr[s], vcol)),
                  pl.BlockSpec((None, IDX_HEADS, t, IDX_DH), lambda bb, s, qr, kr: (bb, 0, qr[s], 0)),
                  pl.BlockSpec((None, t, IDX_DH), lambda bb, s, qr, kr: (bb, kr[s], 0)),
                  pl.BlockSpec((t, IDX_HEADS), qrow),
                  pl.BlockSpec((t, 1), qrow),
                  pl.BlockSpec((t, 1), qrow),
                  pl.BlockSpec((3, DSA_HEADS, t, t), lambda bb, s, qr, kr: (0, 0, 0, 0))],
        out_specs=pl.BlockSpec((t, D_DSA), qrow),
        scratch_shapes=[pltpu.VMEM((DSA_HEADS, t, 1), F32), pltpu.VMEM((DSA_HEADS, t, 1), F32),
                        pltpu.VMEM((t, D_DSA), F32)])
    return pl.pallas_call(
        functools.partial(_dsa_attn_kernel, lp=lp),
        grid_spec=grid_spec,
        out_shape=jax.ShapeDtypeStruct((n, D_DSA), BF16),
        compiler_params=_params(("parallel", "arbitrary")),
        name="dsa_attention",
    )(qb_of, kt_of, far, big, big, big, qi, kidx, w, thr, lim, band)


def _router_kernel(h_ref, w_ref, b_ref, info_ref, cnt_ref, carry_ref):
    i = pl.program_id(0)
    tm = h_ref.shape[0]

    @pl.when(i == 0)
    def _():
        carry_ref[...] = jnp.zeros_like(carry_ref)

    xh, xm, xl = _split3(h_ref[...])
    w = w_ref[...]
    wh, wm, wl = _split3(w)
    logits = (_dot(xh, wh) + (_dot(xh, wm) + _dot(xm, wh)) + (_dot(xm, wm) + _dot(xl, wh) + _dot(xh, wl))
              + b_ref[...])
    lane = lax.broadcasted_iota(I32, (tm, LANES), 1)
    logits = jnp.where(lane < N_EXPERTS, logits, -jnp.inf)
    v1 = jnp.max(logits, axis=1, keepdims=True)
    e1 = jnp.min(jnp.where(logits == v1, lane, LANES), axis=1, keepdims=True)
    rest = jnp.where(lane == e1, -jnp.inf, logits)
    v2 = jnp.max(rest, axis=1, keepdims=True)
    e2 = jnp.min(jnp.where(rest == v2, lane, LANES), axis=1, keepdims=True)
    ex = jnp.exp(v2 - v1)
    g1 = 1.0 / (1.0 + ex)
    g2 = ex / (1.0 + ex)
    onehot = jnp.where((lane == e1) | (lane == e2), 1.0, 0.0)
    row = lax.broadcasted_iota(I32, (tm, tm), 0)
    col = lax.broadcasted_iota(I32, (tm, tm), 1)
    before = jnp.where(col < row, 1.0, 0.0).astype(BF16)
    rank = _dot(before, onehot.astype(BF16)) + carry_ref[...]
    r1 = jnp.sum(jnp.where(lane == e1, rank, 0.0), axis=1, keepdims=True)
    r2 = jnp.sum(jnp.where(lane == e2, rank, 0.0), axis=1, keepdims=True)
    carry_ref[...] += jnp.sum(onehot, axis=0, keepdims=True)
    cnt_ref[...] = carry_ref[...]
    info = jnp.where(lane == 0, e1.astype(F32), 0.0)
    info = jnp.where(lane == 1, e2.astype(F32), info)
    info = jnp.where(lane == 2, r1, info)
    info = jnp.where(lane == 3, r2, info)
    info = jnp.where(lane == 4, g1, info)
    info = jnp.where(lane == 5, g2, info)
    info_ref[...] = info


def _router(h, router_w, router_b, tm):
    n, d = h.shape
    wpad = jnp.zeros((d, LANES), F32).at[:, :N_EXPERTS].set(router_w.astype(F32))
    bpad = jnp.zeros((1, LANES), F32).at[0, :N_EXPERTS].set(router_b.astype(F32))
    return pl.pallas_call(
        _router_kernel,
        grid=(n // tm,),
        in_specs=[pl.BlockSpec((tm, d), lambda i: (i, 0)),
                  pl.BlockSpec((d, LANES), lambda i: (0, 0)),
                  pl.BlockSpec((1, LANES), lambda i: (0, 0))],
        out_specs=[pl.BlockSpec((tm, LANES), lambda i: (i, 0)), pl.BlockSpec((1, LANES), lambda i: (0, 0))],
        out_shape=[jax.ShapeDtypeStruct((n, LANES), F32), jax.ShapeDtypeStruct((1, LANES), F32)],
        scratch_shapes=[pltpu.VMEM((1, LANES), F32)],
        compiler_params=_params(("arbitrary",)),
        name="router",
    )(h, wpad, bpad)


def _dispatch_kernel(d1_ref, d2_ref, x_ref, xg_in_ref, xg_ref, sem, *, rows):
    del xg_in_ref
    base = pl.program_id(0) * rows

    def copy(t, dest):
        return pltpu.make_async_copy(x_ref.at[pl.ds(t, 1)], xg_ref.at[pl.ds(dest, 1)], sem)

    def issue(r, carry):
        t = base + r
        copy(t, d1_ref[t]).start()
        copy(t, d2_ref[t]).start()
        return carry

    def drain(r, carry):
        copy(0, 0).wait()
        copy(0, 0).wait()
        return carry

    lax.fori_loop(0, rows, issue, 0)
    lax.fori_loop(0, rows, drain, 0)


def _dispatch(x, d1, d2, cap, rows):
    n, d = x.shape
    grid_spec = pltpu.PrefetchScalarGridSpec(
        num_scalar_prefetch=2,
        grid=(n // rows,),
        in_specs=[pl.BlockSpec(memory_space=pl.ANY), pl.BlockSpec(memory_space=pl.ANY)],
        out_specs=pl.BlockSpec(memory_space=pl.ANY),
        scratch_shapes=[pltpu.SemaphoreType.DMA(())])
    return pl.pallas_call(
        functools.partial(_dispatch_kernel, rows=rows),
        grid_spec=grid_spec,
        out_shape=jax.ShapeDtypeStruct((cap, d), x.dtype),
        input_output_aliases={3: 0},
        compiler_params=_params(("arbitrary",)),
        name="moe_dispatch",
    )(d1, d2, x, jnp.zeros((cap, d), x.dtype))


def _expert_kernel(be_ref, x_ref, w1_ref, w3_ref, w2_ref, y_ref, xb_ref, acc_ref, *, nf):
    del be_ref
    f = pl.program_id(1)

    @pl.when(f == 0)
    def _():
        xb_ref[...] = x_ref[...].astype(BF16)
        acc_ref[...] = jnp.zeros_like(acc_ref)

    x = xb_ref[...]
    a = _dot(x, w1_ref[...])
    c = _dot(x, w3_ref[...])
    mid = (a * jax.nn.sigmoid(a) * c).astype(BF16)
    acc_ref[...] += _dot(mid, w2_ref[...])

    @pl.when(f == nf - 1)
    def _():
        y_ref[...] = acc_ref[...]


def _experts(xg, blk_expert, w1, w3, w2, tf):
    cap, d = xg.shape
    nf = w1.shape[2] // tf
    grid_spec = pltpu.PrefetchScalarGridSpec(
        num_scalar_prefetch=1,
        grid=(cap // MOE_BLK, nf),
        in_specs=[pl.BlockSpec((MOE_BLK, d), lambda i, f, be: (i, 0)),
                  pl.BlockSpec((None, d, tf), lambda i, f, be: (be[i], 0, f)),
                  pl.BlockSpec((None, d, tf), lambda i, f, be: (be[i], 0, f)),
                  pl.BlockSpec((None, tf, d), lambda i, f, be: (be[i], f, 0))],
        out_specs=pl.BlockSpec((MOE_BLK, d), lambda i, f, be: (i, 0)),
        scratch_shapes=[pltpu.VMEM((MOE_BLK, d), BF16), pltpu.VMEM((MOE_BLK, d), F32)])
    return pl.pallas_call(
        functools.partial(_expert_kernel, nf=nf),
        grid_spec=grid_spec,
        out_shape=jax.ShapeDtypeStruct((cap, d), F32),
        compiler_params=_params(("parallel", "arbitrary")),
        name="moe_experts",
    )(blk_expert, xg, w1.astype(BF16), w3.astype(BF16), w2.astype(BF16))


def _combine_ln_kernel(d1_ref, d2_ref, yb_ref, h_ref, info_ref, g_ref, b_ref, o_ref, y1_ref, y2_ref, sem):
    tm = h_ref.shape[0]
    base = pl.program_id(0) * tm

    def copy(dest, buf, r):
        return pltpu.make_async_copy(yb_ref.at[pl.ds(dest, 1)], buf.at[pl.ds(r, 1)], sem)

    def issue(r, carry):
        copy(d1_ref[base + r], y1_ref, r).start()
        copy(d2_ref[base + r], y2_ref, r).start()
        return carry

    def drain(r, carry):
        copy(0, y1_ref, 0).wait()
        copy(0, y2_ref, 0).wait()
        return carry

    lax.fori_loop(0, tm, issue, 0)
    lax.fori_loop(0, tm, drain, 0)
    info = info_ref[...]
    moe = info[:, 4:5] * y1_ref[...] + info[:, 5:6] * y2_ref[...]
    o_ref[...] = _layer_norm(ALPHA * h_ref[...] + moe, g_ref[...], b_ref[...])


def _combine_ln(yb, d1, d2, h, info, g, b, tm):
    n, d = h.shape
    grid_spec = pltpu.PrefetchScalarGridSpec(
        num_scalar_prefetch=2,
        grid=(n // tm,),
        in_specs=[pl.BlockSpec(memory_space=pl.ANY),
                  pl.BlockSpec((tm, d), lambda i, a, c: (i, 0)),
                  pl.BlockSpec((tm, LANES), lambda i, a, c: (i, 0)),
                  pl.BlockSpec((1, d), lambda i, a, c: (0, 0)),
                  pl.BlockSpec((1, d), lambda i, a, c: (0, 0))],
        out_specs=pl.BlockSpec((tm, d), lambda i, a, c: (i, 0)),
        scratch_shapes=[pltpu.VMEM((tm, d), F32), pltpu.VMEM((tm, d), F32), pltpu.SemaphoreType.DMA(())])
    return pl.pallas_call(
        _combine_ln_kernel,
        grid_spec=grid_spec,
        out_shape=jax.ShapeDtypeStruct((n, d), F32),
        compiler_params=_params(("arbitrary",)),
        name="moe_combine_ln",
    )(d1, d2, yb, h, info, g.reshape(1, d), b.reshape(1, d))


def _moe_ln(h, router_w, router_b, w1, w3, w2, g, b, tm):
    n, d = h.shape
    info, counts = _router(h, router_w, router_b, tm)
    counts = counts[0, :N_EXPERTS].astype(I32)
    padded = (counts + MOE_BLK - 1) // MOE_BLK * MOE_BLK
    pad_end = jnp.cumsum(padded)
    pad_start = pad_end - padded
    e1, e2 = info[:, 0].astype(I32), info[:, 1].astype(I32)
    d1 = pad_start[e1] + info[:, 2].astype(I32)
    d2 = pad_start[e2] + info[:, 3].astype(I32)
    nblk = -(-(n * TOP_K) // MOE_BLK) + N_EXPERTS
    blk_expert = jnp.minimum(jnp.searchsorted(pad_end, jnp.arange(nblk, dtype=I32) * MOE_BLK, side='right'),
                             N_EXPERTS - 1).astype(I32)
    xg = _dispatch(h, d1, d2, nblk * MOE_BLK, tm)
    yb = _experts(xg, blk_expert, w1, w3, w2, 512)
    return _combine_ln(yb, d1, d2, h, info, g, b, tm)


def kernel(x, meta_tokens, w_in_even, conv_w, s5_a_re, s5_a_im, s5_log_dt, s5_b_re, s5_b_im, s5_c_re, s5_c_im, s5_d, s5_w_glu, w_out_even, ln_mix_even_g, ln_mix_even_b, ffn_w1, ffn_w3, ffn_w2, ln_ffn_even_g, ln_ffn_even_b, w_in_odd, mlstm_b_i, mlstm_b_f, rel_bias_table, w_out_odd, ln_mix_odd_g, ln_mix_odd_b, router_w, router_b, expert_w1, expert_w3, expert_w2, ln_ffn_odd_g, ln_ffn_odd_b):
    bsz, seq, d = x.shape
    topk = min(TOPK_MAX, seq // 4)
    lh = seq + N_META
    lp = -(-lh // SEQ_TILE) * SEQ_TILE
    n = bsz * lp
    tm = _pick(n, (1280, 640, 512, 256))
    tln = _pick(n, (512, 256))
    tseq = _pick(lp, (1280, 640, 256))

    h = jnp.concatenate([jnp.broadcast_to(meta_tokens.astype(x.dtype)[None], (bsz, N_META, d)), x,
                         jnp.zeros((bsz, lp - lh, d), x.dtype)], axis=1).reshape(n, d)
    hb = h.astype(BF16)

    p = _proj(hb, w_in_even[0].astype(BF16), F32, tm, 1280)
    y_conv = _conv_mixer(p, conv_w[0].astype(F32), lp, tseq, 512)
    ops = _s5_operators(s5_a_re[0], s5_a_im[0], s5_log_dt[0], s5_b_re[0], s5_b_im[0], s5_c_re[0], s5_c_im[0], S5_T)
    u = p[:, 3 * D_CONV:].astype(BF16)
    y_lin = _s5_linear(u, ops, bsz, lp)
    y_s5 = _s5_glu(y_lin, p, s5_d[0], s5_w_glu[0], tm)
    h, hb = _outproj_ln(y_conv, y_s5, w_out_even[0], h, ln_mix_even_g[0], ln_mix_even_b[0], tln)
    h, hb = _ffn_ln(h, ffn_w1[0], ffn_w3[0], ffn_w2[0], ln_ffn_even_g[0], ln_ffn_even_b[0], tln, 512)

    w = w_in_odd[0]
    o = np.cumsum([0] + [D_MLSTM] * 4 + [MLSTM_HEADS] * 2 + [D_DSA] * 3 + [IDX_HEADS * IDX_DH, IDX_DH, IDX_HEADS])
    seg = lambda j: w[:, o[j]:o[j + 1]]
    w_big = jnp.concatenate([seg(0), seg(1), seg(2), seg(3), seg(6), seg(7), seg(8), seg(9)], axis=1)
    w_small = jnp.concatenate([seg(4), seg(5), seg(10), seg(11)], axis=1)
    w_small = jnp.pad(w_small, ((0, 0), (0, LANES - w_small.shape[1])))
    big = _proj(hb, w_big.astype(BF16), BF16, tm, 1280)
    small = _proj(hb, w_small.astype(BF16), F32, tm, LANES)
    gate_bias = jnp.zeros((1, LANES), F32).at[0, :MLSTM_HEADS].set(mlstm_b_i[0].astype(F32))
    gate_bias = gate_bias.at[0, MLSTM_HEADS:2 * MLSTM_HEADS].set(mlstm_b_f[0].astype(F32))
    y_m = _mlstm(big, small, gate_bias, bsz, lp)
    qi = big[:, 7 * D_DSA:].reshape(bsz, lp, IDX_HEADS, IDX_DH).transpose(0, 2, 1, 3)
    kidx = small[:, 2 * MLSTM_HEADS:2 * MLSTM_HEADS + IDX_DH].astype(BF16).reshape(bsz, lp, IDX_DH)
    widx = small[:, 2 * MLSTM_HEADS + IDX_DH:2 * MLSTM_HEADS + IDX_DH + IDX_HEADS]
    thr, lim = _dsa_select(qi, kidx, widx, bsz, lp, topk)
    y_d = _dsa_attention(big, qi, kidx, widx, thr, lim, rel_bias_table, bsz, lp)
    h, hb = _outproj_ln(y_m, y_d, w_out_odd[0], h, ln_mix_odd_g[0], ln_mix_odd_b[0], tln)
    h = _moe_ln(h, router_w[0], router_b[0], expert_w1[0], expert_w3[0], expert_w2[0],
                ln_ffn_odd_g[0], ln_ffn_odd_b[0], tln)
    return h.reshape(bsz, lp, d)[:, N_META:lh]
```

```python
import functools
import math

import jax
import jax.numpy as jnp
import numpy as np
from jax import lax
from jax.experimental import pallas as pl
from jax.experimental.pallas import tpu as pltpu

F32 = jnp.float32
BF16 = jnp.bfloat16
I32 = jnp.int32

D_MODEL = 2048
DEPTH = 2
CHUNK = 64
N_META = 16
CONV_WIDTH = 3
D_CONV = 1536
S5_CH = 16
S5_STATE = 64
D_SSM = 512
S5_GROUPS = D_SSM // S5_CH
MLSTM_HEADS = 4
MLSTM_DH = 256
D_MLSTM = MLSTM_HEADS * MLSTM_DH
DSA_HEADS = 8
DSA_DH = 128
D_DSA = DSA_HEADS * DSA_DH
IDX_HEADS = 8
IDX_DH = 64
TOPK_MAX = 256
NUM_BUCKETS = 32
MAX_DISTANCE = 128
D_FF = 5632
N_EXPERTS = 8
TOP_K = 2
D_EXPERT = 5632
ALPHA = (2 * DEPTH) ** 0.25
LN_EPS = 1e-5

LANES = 128
SEQ_TILE = 256
S5_T = 64
MOE_BLK = 512
VMEM_LIMIT = 56 * 1024 * 1024
NEG = -1e30
INT_MIN = -(2 ** 31)


def _params(sem, vmem=VMEM_LIMIT):
    return pltpu.CompilerParams(dimension_semantics=sem, vmem_limit_bytes=vmem)


def _pick(n, candidates):
    for c in candidates:
        if n % c == 0:
            return c
    raise ValueError(f"no tile in {candidates} divides {n}")


def _dot(a, b):
    return jnp.dot(a, b, preferred_element_type=F32)


def _dot_nt(a, b):
    return lax.dot_general(a, b, (((1,), (1,)), ((), ())), preferred_element_type=F32)


def _split3(x):
    hi = x.astype(BF16)
    r = x - hi.astype(F32)
    mid = r.astype(BF16)
    lo = (r - mid.astype(F32)).astype(BF16)
    return hi, mid, lo


def _layer_norm(z, g, b):
    mu = jnp.mean(z, axis=-1, keepdims=True)
    zc = z - mu
    var = jnp.mean(zc * zc, axis=-1, keepdims=True)
    return zc * lax.rsqrt(var + LN_EPS) * g + b


def _proj_kernel(x_ref, w_ref, o_ref):
    o_ref[...] = _dot(x_ref[...], w_ref[...]).astype(o_ref.dtype)


def _proj(x, w, out_dtype, tm, tn):
    n, k = x.shape
    m = w.shape[1]
    return pl.pallas_call(
        _proj_kernel,
        grid=(n // tm, m // tn),
        in_specs=[pl.BlockSpec((tm, k), lambda i, j: (i, 0)),
                  pl.BlockSpec((k, tn), lambda i, j: (0, j))],
        out_specs=pl.BlockSpec((tm, tn), lambda i, j: (i, j)),
        out_shape=jax.ShapeDtypeStruct((n, m), out_dtype),
        compiler_params=_params(("parallel", "parallel")),
        name="proj",
    )(x, w)


def _outproj_ln_kernel(ya_ref, yb_ref, wa_ref, wb_ref, h_ref, g_ref, b_ref, o_ref, ob_ref):
    mix = _dot(ya_ref[...], wa_ref[...]) + _dot(yb_ref[...], wb_ref[...])
    out = _layer_norm(ALPHA * h_ref[...] + mix, g_ref[...], b_ref[...])
    o_ref[...] = out
    ob_ref[...] = out.astype(BF16)


def _outproj_ln(ya, yb, w, h, g, b, tm):
    n, d = h.shape
    ka, kb = ya.shape[1], yb.shape[1]
    wa, wb = w[:ka].astype(BF16), w[ka:].astype(BF16)
    row = lambda i: (i, 0)
    fixed = lambda i: (0, 0)
    return pl.pallas_call(
        _outproj_ln_kernel,
        grid=(n // tm,),
        in_specs=[pl.BlockSpec((tm, ka), row), pl.BlockSpec((tm, kb), row),
                  pl.BlockSpec((ka, d), fixed), pl.BlockSpec((kb, d), fixed),
                  pl.BlockSpec((tm, d), row),
                  pl.BlockSpec((1, d), fixed), pl.BlockSpec((1, d), fixed)],
        out_specs=[pl.BlockSpec((tm, d), row), pl.BlockSpec((tm, d), row)],
        out_shape=[jax.ShapeDtypeStruct((n, d), F32), jax.ShapeDtypeStruct((n, d), BF16)],
        compiler_params=_params(("parallel",)),
        name="outproj_ln",
    )(ya, yb, wa, wb, h, g.reshape(1, d), b.reshape(1, d))


def _conv_kernel(b_ref, c_ref, x_ref, ch_ref, xh_ref, w_ref, y_ref, *, tiles_per_seq):
    i = pl.program_id(0)
    z = c_ref[...] * x_ref[...]
    zh = ch_ref[...] * xh_ref[...]
    zh = jnp.where(i % tiles_per_seq == 0, 0.0, zh)
    r = lax.broadcasted_iota(I32, z.shape, 0)
    z1 = jnp.where(r == 0, zh[7:8], pltpu.roll(z, 1, 0))
    z2 = jnp.where(r == 0, zh[6:7], jnp.where(r == 1, zh[7:8], pltpu.roll(z, 2, 0)))
    w = w_ref[...]
    conv = w[0:1] * z2 + w[1:2] * z1 + w[2:3] * z
    y_ref[...] = (b_ref[...] * conv).astype(y_ref.dtype)


def _conv_mixer(p, conv_w, lp, tm, tc):
    n = p.shape[0]
    ncb = D_CONV // tc
    hb = tm // 8
    body = functools.partial(_conv_kernel, tiles_per_seq=lp // tm)
    return pl.pallas_call(
        body,
        grid=(n // tm, ncb),
        in_specs=[pl.BlockSpec((tm, tc), lambda i, j: (i, j)),
                  pl.BlockSpec((tm, tc), lambda i, j: (i, ncb + j)),
                  pl.BlockSpec((tm, tc), lambda i, j: (i, 2 * ncb + j)),
                  pl.BlockSpec((8, tc), lambda i, j: (jnp.maximum(i * hb - 1, 0), ncb + j)),
                  pl.BlockSpec((8, tc), lambda i, j: (jnp.maximum(i * hb - 1, 0), 2 * ncb + j)),
                  pl.BlockSpec((CONV_WIDTH, tc), lambda i, j: (0, j))],
        out_specs=pl.BlockSpec((tm, tc), lambda i, j: (i, j)),
        out_shape=jax.ShapeDtypeStruct((n, D_CONV), BF16),
        compiler_params=_params(("parallel", "parallel")),
        name="conv_mixer",
    )(p, p, p, p, p, conv_w)


def _s5_operators(a_re, a_im, log_dt, b_re, b_im, c_re, c_im, t):
    a = lax.complex(a_re.astype(F32), a_im.astype(F32))
    dt = jnp.exp(log_dt.astype(F32))[:, None]
    a_bar = jnp.exp(dt * a)
    b_bar = ((a_bar - 1.0) / a)[:, :, None] * lax.complex(b_re.astype(F32), b_im.astype(F32))
    c = lax.complex(c_re.astype(F32), c_im.astype(F32))
    lag = jnp.arange(t + 1, dtype=F32)[:, None, None]
    pw = jnp.exp(lag * (dt * a)[None])
    kern = jnp.real(jnp.einsum('gcp,lgp,gpd->glcd', c, pw[:t], b_bar))
    i_idx = jnp.arange(t)[None, :]
    j_idx = jnp.arange(t)[:, None]
    lagm = i_idx - j_idx
    blocks = jnp.where((lagm >= 0)[None, :, :, None, None], kern[:, jnp.maximum(lagm, 0)], 0.0)
    g = a.shape[0]
    toep = blocks.transpose(0, 1, 4, 2, 3).reshape(g, t * S5_CH, t * S5_CH)
    wc = pw[:t][::-1].transpose(1, 0, 2)[:, :, None, :] * b_bar.transpose(0, 2, 1)[:, None, :, :]
    wc = wc.reshape(g, t * S5_CH, S5_STATE)
    w_in = jnp.concatenate([jnp.real(wc), jnp.imag(wc)], axis=-1)
    cp = c.transpose(0, 2, 1)[:, :, None, :] * pw[1:t + 1].transpose(1, 2, 0)[:, :, :, None]
    cp = cp.reshape(g, S5_STATE, t * S5_CH)
    v_out = jnp.concatenate([jnp.real(cp), -jnp.imag(cp)], axis=1)
    a_t = jnp.stack([jnp.real(pw[t]), jnp.imag(pw[t])], axis=1)
    return toep.astype(BF16), w_in.astype(BF16), v_out.astype(BF16), a_t


def _s5_kernel(u_ref, toep_ref, win_ref, vout_ref, at_ref, y_ref, er_ref, ei_ref, hr_ref, hi_ref, *, nbatch, nc):
    u = u_ref[...]
    e = _dot(u, win_ref[...])
    er_ref[...] = e[:, :S5_STATE]
    ei_ref[...] = e[:, S5_STATE:]
    at = at_ref[...]
    ar, ai = at[0:1], at[1:2]

    def step(c, carry):
        out = []
        for b in range(nbatch):
            hr, hi = carry[2 * b], carry[2 * b + 1]
            row = b * nc + c
            hr_ref[pl.ds(row, 1), :] = hr
            hi_ref[pl.ds(row, 1), :] = hi
            er = er_ref[pl.ds(row, 1), :]
            ei = ei_ref[pl.ds(row, 1), :]
            out += [ar * hr - ai * hi + er, ar * hi + ai * hr + ei]
        return tuple(out)

    zero = jnp.zeros((1, S5_STATE), F32)
    lax.fori_loop(0, nc, step, (zero,) * (2 * nbatch))
    vout = vout_ref[...]
    y = _dot(u, toep_ref[...])
    for part, rows in ((hr_ref[...], vout[:S5_STATE]), (hi_ref[...], vout[S5_STATE:])):
        hi_, mid_, _ = _split3(part)
        y += _dot(hi_, rows) + _dot(mid_, rows)
    y_ref[...] = y


def _s5_linear(u, ops, nbatch, lp):
    toep, w_in, v_out, a_t = ops
    n = u.shape[0]
    t = S5_T
    nc = lp // t
    rows = nbatch * nc
    width = t * S5_CH
    ug = u.reshape(rows, t, S5_GROUPS, S5_CH).transpose(2, 0, 1, 3).reshape(S5_GROUPS, rows, width)
    grp = lambda g: (g, 0, 0)
    yg = pl.pallas_call(
        functools.partial(_s5_kernel, nbatch=nbatch, nc=nc),
        grid=(S5_GROUPS,),
        in_specs=[pl.BlockSpec((None, rows, width), grp),
                  pl.BlockSpec((None, width, width), grp),
                  pl.BlockSpec((None, width, 2 * S5_STATE), grp),
                  pl.BlockSpec((None, 2 * S5_STATE, width), grp),
                  pl.BlockSpec((None, 2, S5_STATE), grp)],
        out_specs=pl.BlockSpec((None, rows, width), grp),
        out_shape=jax.ShapeDtypeStruct((S5_GROUPS, rows, width), F32),
        scratch_shapes=[pltpu.VMEM((rows, S5_STATE), F32)] * 4,
        compiler_params=_params(("parallel",)),
        name="s5_linear",
    )(ug, toep, w_in, v_out, a_t)
    return yg.reshape(S5_GROUPS, rows, t, S5_CH).transpose(1, 2, 0, 3).reshape(n, D_SSM)


def _s5_glu_kernel(y_ref, u_ref, d_ref, w_ref, o_ref):
    y = y_ref[...] + d_ref[...] * u_ref[...]
    y = jax.nn.gelu(y)
    hi, mid, _ = _split3(y)
    w = w_ref[...]
    gate = _dot(hi, w) + _dot(mid, w)
    o_ref[...] = (y * jax.nn.sigmoid(gate)).astype(o_ref.dtype)


def _s5_glu(y_lin, p, d_skip, w_glu, tm):
    n = y_lin.shape[0]
    ucol = (3 * D_CONV) // D_SSM
    wbd = jnp.einsum('gcd,gh->gdhc', w_glu.astype(F32), jnp.eye(S5_GROUPS, dtype=F32)).reshape(D_SSM, D_SSM)
    return pl.pallas_call(
        _s5_glu_kernel,
        grid=(n // tm,),
        in_specs=[pl.BlockSpec((tm, D_SSM), lambda i: (i, 0)),
                  pl.BlockSpec((tm, D_SSM), lambda i: (i, ucol)),
                  pl.BlockSpec((1, D_SSM), lambda i: (0, 0)),
                  pl.BlockSpec((D_SSM, D_SSM), lambda i: (0, 0))],
        out_specs=pl.BlockSpec((tm, D_SSM), lambda i: (i, 0)),
        out_shape=jax.ShapeDtypeStruct((n, D_SSM), BF16),
        compiler_params=_params(("parallel",)),
        name="s5_glu",
    )(y_lin, p, d_skip.reshape(1, D_SSM).astype(F32), wbd.astype(BF16))


def _ffn_ln_kernel(h_ref, w1_ref, w3_ref, w2_ref, g_ref, b_ref, o_ref, ob_ref, xb_ref, acc_ref, *, nf):
    f = pl.program_id(1)

    @pl.when(f == 0)
    def _():
        xb_ref[...] = h_ref[...].astype(BF16)
        acc_ref[...] = jnp.zeros_like(acc_ref)

    x = xb_ref[...]
    a = _dot(x, w1_ref[...])
    c = _dot(x, w3_ref[...])
    mid = (a * jax.nn.sigmoid(a) * c).astype(BF16)
    acc_ref[...] += _dot(mid, w2_ref[...])

    @pl.when(f == nf - 1)
    def _():
        out = _layer_norm(ALPHA * h_ref[...] + acc_ref[...], g_ref[...], b_ref[...])
        o_ref[...] = out
        ob_ref[...] = out.astype(BF16)


def _ffn_ln(h, w1, w3, w2, g, b, tm, tf):
    n, d = h.shape
    nf = w1.shape[1] // tf
    return pl.pallas_call(
        functools.partial(_ffn_ln_kernel, nf=nf),
        grid=(n // tm, nf),
        in_specs=[pl.BlockSpec((tm, d), lambda i, f: (i, 0)),
                  pl.BlockSpec((d, tf), lambda i, f: (0, f)),
                  pl.BlockSpec((d, tf), lambda i, f: (0, f)),
                  pl.BlockSpec((tf, d), lambda i, f: (f, 0)),
                  pl.BlockSpec((1, d), lambda i, f: (0, 0)),
                  pl.BlockSpec((1, d), lambda i, f: (0, 0))],
        out_specs=[pl.BlockSpec((tm, d), lambda i, f: (i, 0)), pl.BlockSpec((tm, d), lambda i, f: (i, 0))],
        out_shape=[jax.ShapeDtypeStruct((n, d), F32), jax.ShapeDtypeStruct((n, d), BF16)],
        scratch_shapes=[pltpu.VMEM((tm, d), BF16), pltpu.VMEM((tm, d), F32)],
        compiler_params=_params(("parallel", "arbitrary")),
        name="ffn_ln",
    )(h, w1.astype(BF16), w3.astype(BF16), w2.astype(BF16), g.reshape(1, d), b.reshape(1, d))


def _log_sigmoid(x):
    return jnp.minimum(x, 0.0) - jnp.log(1.0 + jnp.exp(-jnp.abs(x)))


def _mlstm_kernel(q_ref, k_ref, v_ref, o_ref, g_ref, gb_ref, y_ref, ct_ref, n_ref, m_ref):
    c = pl.program_id(1)
    t = q_ref.shape[0]
    dh = MLSTM_DH

    @pl.when(c == 0)
    def _():
        ct_ref[...] = jnp.zeros_like(ct_ref)
        n_ref[...] = jnp.zeros_like(n_ref)
        m_ref[...] = jnp.zeros_like(m_ref)

    gates = g_ref[...] + gb_ref[...]
    lf = _log_sigmoid(gates)
    row = lax.broadcasted_iota(I32, (t, t), 0)
    col = lax.broadcasted_iota(I32, (t, t), 1)
    tril = col <= row
    tril_b = jnp.where(tril, 1.0, 0.0).astype(BF16)
    hi, mid, lo = _split3(lf)
    bcum = _dot(tril_b, hi) + _dot(tril_b, mid) + _dot(tril_b, lo)
    gates_t = gates.T
    bcum_t = bcum.T
    for h in range(MLSTM_HEADS):
        sl = slice(h * dh, (h + 1) * dh)
        fh = MLSTM_HEADS + h
        li_row, li_col = gates_t[h:h + 1, :], gates[:, h:h + 1]
        bc_row, bc_col = bcum_t[fh:fh + 1, :], bcum[:, fh:fh + 1]
        m_prev = m_ref[h:h + 1, 0:1]
        dmat = jnp.where(tril, bc_col - bc_row + li_row, -jnp.inf)
        m_inter = bc_col + m_prev
        m_t = jnp.maximum(m_inter, jnp.max(dmat, axis=1, keepdims=True))
        q = q_ref[:, sl]
        k = k_ref[:, sl] * (dh ** -0.5)
        v = v_ref[:, sl]
        s = _dot_nt(q, k) * jnp.exp(dmat - m_t)
        w_inter = jnp.exp(m_inter - m_t)
        ct = ct_ref[h]
        nvec = n_ref[h:h + 1, :]
        num = _dot(s.astype(BF16), v) + w_inter * _dot(q, ct.astype(BF16))
        den = jnp.sum(s, axis=1, keepdims=True) + w_inter * jnp.sum(q.astype(F32) * nvec, axis=1, keepdims=True)
        h_out = num / jnp.maximum(jnp.abs(den), jnp.exp(-m_t))
        y_ref[:, sl] = (jax.nn.sigmoid(o_ref[:, sl].astype(F32)) * h_out).astype(y_ref.dtype)
        b_last = bc_col[t - 1:t, :]
        m_new = jnp.maximum(b_last + m_prev, jnp.max(b_last - bc_row + li_row, axis=1, keepdims=True))
        decay = jnp.exp(b_last + m_prev - m_new)
        kw = k.astype(F32) * jnp.exp(b_last - bc_col + li_col - m_new)
        ct_ref[h] = decay * ct + _dot(kw.T.astype(BF16), v)
        n_ref[h:h + 1, :] = decay * nvec + jnp.sum(kw, axis=0, keepdims=True)
        m_ref[h:h + 1, :] = jnp.broadcast_to(m_new, (1, LANES))


def _mlstm(big, small, gate_bias, nbatch, lp):
    n = big.shape[0]
    t = SEQ_TILE
    nc = lp // t
    blk = lambda j: pl.BlockSpec((t, D_MLSTM), lambda b, c: (b * nc + c, j))
    return pl.pallas_call(
        _mlstm_kernel,
        grid=(nbatch, nc),
        in_specs=[blk(0), blk(1), blk(2), blk(3),
                  pl.BlockSpec((t, LANES), lambda b, c: (b * nc + c, 0)),
                  pl.BlockSpec((1, LANES), lambda b, c: (0, 0))],
        out_specs=pl.BlockSpec((t, D_MLSTM), lambda b, c: (b * nc + c, 0)),
        out_shape=jax.ShapeDtypeStruct((n, D_MLSTM), BF16),
        scratch_shapes=[pltpu.VMEM((MLSTM_HEADS, MLSTM_DH, MLSTM_DH), F32),
                        pltpu.VMEM((8, MLSTM_DH), F32),
                        pltpu.VMEM((8, LANES), F32)],
        compiler_params=_params(("parallel", "arbitrary")),
        name="mlstm",
    )(big, big, big, big, small, gate_bias)


def _chunk_end(pos):
    return jnp.where(pos < N_META, N_META, N_META + CHUNK + CHUNK * ((pos - N_META) >> 6))


def _t5_bucket(rel):
    half = NUM_BUCKETS // 2
    max_exact = half // 2
    ret = jnp.where(rel > 0, half, 0)
    n = jnp.abs(rel)
    nf = jnp.maximum(n, 1).astype(F32)
    large = max_exact + (jnp.log(nf / max_exact) / math.log(MAX_DISTANCE / max_exact) * (half - max_exact)).astype(I32)
    large = jnp.minimum(large, half - 1)
    return ret + jnp.where(n < max_exact, n, large)


def _index_keys(qi_ref, kidx, wi):
    score = None
    for h in range(IDX_HEADS):
        s = _dot_nt(qi_ref[h], kidx) * (IDX_DH ** -0.5)
        term = wi[:, h:h + 1] * jnp.maximum(s, 0.0)
        score = term if score is None else score + term
    bits = lax.bitcast_convert_type(score, I32)
    return bits ^ ((bits >> 31) & 0x7FFFFFFF)


def _dsa_select_kernel(qi_ref, kidx_ref, w_ref, mask_ref, keys_ref, lim_ref, *, lp, topk):
    qb = SEQ_TILE
    b = pl.program_id(1)
    nkt = jnp.minimum(b + 2, lp // qb)
    wi = w_ref[...] * (IDX_HEADS ** -0.5)
    end = _chunk_end(b * qb + lax.broadcasted_iota(I32, (qb, 1), 0))
    lane = lax.broadcasted_iota(I32, (qb, qb), 1)

    def fill(kt, carry):
        off = pl.multiple_of(kt * qb, qb)
        keys = _index_keys(qi_ref, kidx_ref[pl.ds(off, qb), :], wi)
        keys_ref[:, pl.ds(off, qb)] = jnp.where(off + lane < end, keys, INT_MIN)
        return carry

    lax.fori_loop(0, nkt, fill, 0)

    def count(pred):
        def body(kt, acc):
            off = pl.multiple_of(kt * qb, qb)
            hit = jnp.where(pred(keys_ref[:, pl.ds(off, qb)], off + lane), 1, 0)
            for j in range(qb // LANES):
                acc = acc + hit[:, j * LANES:(j + 1) * LANES]
            return acc
        acc = lax.fori_loop(0, nkt, body, jnp.zeros((qb, LANES), I32))
        return jnp.sum(acc, axis=1, keepdims=True)

    c0 = count(lambda kk, _: kk >= 0)
    thr0 = jnp.where(c0 >= topk, 0, INT_MIN)

    def bit_round(i, thr):
        cand = thr | jnp.left_shift(jnp.int32(1), 30 - i)
        cnt = count(lambda kk, _: kk >= cand)
        return jnp.where(cnt >= topk, cand, thr)

    thr = lax.fori_loop(0, 31, bit_round, thr0)
    need = topk - count(lambda kk, _: kk > thr)
    tie = (count(lambda kk, _: kk >= thr) > topk) & (thr != INT_MIN)
    lim_ref[...] = jnp.full((qb, 1), lp, I32)

    @pl.when(jnp.max(jnp.where(tie, 1, 0)) > 0)
    def _():
        def idx_round(i, x):
            cand = x | jnp.left_shift(jnp.int32(1), 14 - i)
            cnt = count(lambda kk, cc: (kk == thr) & (cc < cand))
            return jnp.where(cnt < need, cand, x)
        x = lax.fori_loop(0, 15, idx_round, jnp.zeros((qb, 1), I32))
        lim_ref[...] = jnp.where(tie, x, lp)

    lim = lim_ref[...]

    def emit(kt, carry):
        off = pl.multiple_of(kt * qb, qb)
        kk = keys_ref[:, pl.ds(off, qb)]
        sel = (kk > thr) | ((kk == thr) & (off + lane <= lim))
        mask_ref[:, pl.ds(off, qb)] = jnp.where(sel, 0.0, NEG).astype(BF16)
        return carry

    lax.fori_loop(0, nkt, emit, 0)

    def blank(kt, carry):
        mask_ref[:, pl.ds(pl.multiple_of(kt * qb, qb), qb)] = jnp.full((qb, qb), NEG, BF16)
        return carry

    lax.fori_loop(nkt, lp // qb, blank, 0)


def _dsa_select(qi, kidx, w, nbatch, lp, topk):
    qb = SEQ_TILE
    nqb = lp // qb
    assert lp < 2 ** 15
    return pl.pallas_call(
        functools.partial(_dsa_select_kernel, lp=lp, topk=topk),
        grid=(nbatch, nqb),
        in_specs=[pl.BlockSpec((None, IDX_HEADS, qb, IDX_DH), lambda bb, b: (bb, 0, b, 0)),
                  pl.BlockSpec((None, lp, IDX_DH), lambda bb, b: (bb, 0, 0)),
                  pl.BlockSpec((qb, IDX_HEADS), lambda bb, b: (bb * nqb + b, 0))],
        out_specs=pl.BlockSpec((qb, lp), lambda bb, b: (bb * nqb + b, 0)),
        out_shape=jax.ShapeDtypeStruct((nbatch * lp, lp), BF16),
        scratch_shapes=[pltpu.VMEM((qb, lp), I32), pltpu.VMEM((qb, 1), I32)],
        compiler_params=_params(("parallel", "parallel")),
        name="dsa_select",
    )(qi, kidx, w)


def _dsa_attn_kernel(qb_ref, kt_ref, q_ref, k_ref, v_ref, mask_ref, band_ref, far_ref, y_ref,
                     s_ref, m_ref, l_ref, acc_ref, *, lp):
    t = SEQ_TILE
    step = pl.program_id(1)
    b = qb_ref[step]
    kt = kt_ref[step]
    last_kt = jnp.minimum(b + 1, lp // t - 1)

    @pl.when(kt == 0)
    def _():
        m_ref[...] = jnp.full_like(m_ref, NEG)
        l_ref[...] = jnp.zeros_like(l_ref)
        acc_ref[...] = jnp.zeros_like(acc_ref)

    heads = [slice(h * DSA_DH, (h + 1) * DSA_DH) for h in range(DSA_HEADS)]
    for h, sl in enumerate(heads):
        s_ref[h] = _dot_nt(q_ref[:, sl], k_ref[:, sl])
    mask = mask_ref[...].astype(F32)

    def update(z, shift):
        m_prev = m_ref[...]
        m_next = jnp.maximum(m_prev, jnp.max(z, axis=2, keepdims=True) + shift)
        p = jnp.exp(z - jnp.concatenate([m_next - shift] * (t // LANES), axis=2))
        alpha = jnp.exp(m_prev - m_next)
        l_ref[...] = alpha * l_ref[...] + jnp.sum(p, axis=2, keepdims=True)
        m_ref[...] = m_next
        for h, sl in enumerate(heads):
            acc_ref[:, sl] = alpha[h] * acc_ref[:, sl] + _dot(p[h].astype(BF16), v_ref[:, sl])

    near = kt >= b - 1

    @pl.when(near)
    def _():
        bias = band_ref[kt - b + 1] + mask[None]
        update(s_ref[...] * (DSA_DH ** -0.5) + bias, jnp.zeros((DSA_HEADS, 1, LANES), F32))

    @pl.when(jnp.logical_not(near))
    def _():
        update(s_ref[...] * (DSA_DH ** -0.5) + mask[None], far_ref[...])

    @pl.when(kt == last_kt)
    def _():
        for h, sl in enumerate(heads):
            y_ref[:, sl] = (acc_ref[:, sl] / l_ref[h]).astype(y_ref.dtype)


def _dsa_bias_band(rel_bias_table):
    t = SEQ_TILE
    i = jnp.arange(t, dtype=I32)[:, None]
    c = jnp.arange(3 * t, dtype=I32)[None, :] - t
    bias = rel_bias_table.astype(F32)[_t5_bucket(c - i)]
    band = jnp.where((c < _chunk_end(i))[:, :, None], bias, NEG)
    band = band.reshape(t, 3, t, DSA_HEADS).transpose(1, 3, 0, 2)
    far = rel_bias_table.astype(F32)[_t5_bucket(jnp.int32(-(t + 1)))]
    return band, jnp.broadcast_to(far[:, None, None], (DSA_HEADS, 1, LANES))


def _dsa_attention(big, mask, rel_bias_table, nbatch, lp):
    t = SEQ_TILE
    nqb = lp // t
    n = big.shape[0]
    pairs = [(b, kt) for b in range(nqb) for kt in range(min(b + 2, nqb))]
    qb_of = jnp.asarray(np.array([p[0] for p in pairs], np.int32))
    kt_of = jnp.asarray(np.array([p[1] for p in pairs], np.int32))
    band, far = _dsa_bias_band(rel_bias_table)
    qcol, kcol, vcol = 4, 5, 6
    qrow = lambda bb, s, qr, kr: (bb * nqb + qr[s], 0)
    grid_spec = pltpu.PrefetchScalarGridSpec(
        num_scalar_prefetch=2,
        grid=(nbatch, len(pairs)),
        in_specs=[pl.BlockSpec((t, D_DSA), lambda bb, s, qr, kr: (bb * nqb + qr[s], qcol)),
                  pl.BlockSpec((t, D_DSA), lambda bb, s, qr, kr: (bb * nqb + kr[s], kcol)),
                  pl.BlockSpec((t, D_DSA), lambda bb, s, qr, kr: (bb * nqb + kr[s], vcol)),
                  pl.BlockSpec((t, t), lambda bb, s, qr, kr: (bb * nqb + qr[s], kr[s])),
                  pl.BlockSpec((3, DSA_HEADS, t, t), lambda bb, s, qr, kr: (0, 0, 0, 0)),
                  pl.BlockSpec((DSA_HEADS, 1, LANES), lambda bb, s, qr, kr: (0, 0, 0))],
        out_specs=pl.BlockSpec((t, D_DSA), qrow),
        scratch_shapes=[pltpu.VMEM((DSA_HEADS, t, t), F32),
                        pltpu.VMEM((DSA_HEADS, t, LANES), F32), pltpu.VMEM((DSA_HEADS, t, LANES), F32),
                        pltpu.VMEM((t, D_DSA), F32)])
    return pl.pallas_call(
        functools.partial(_dsa_attn_kernel, lp=lp),
        grid_spec=grid_spec,
        out_shape=jax.ShapeDtypeStruct((n, D_DSA), BF16),
        compiler_params=_params(("parallel", "arbitrary")),
        name="dsa_attention",
    )(qb_of, kt_of, big, big, big, mask, band, far)


def _router_kernel(h_ref, w_ref, b_ref, info_ref, cnt_ref, carry_ref):
    i = pl.program_id(0)
    tm = h_ref.shape[0]

    @pl.when(i == 0)
    def _():
        carry_ref[...] = jnp.zeros_like(carry_ref)

    xh, xm, xl = _split3(h_ref[...])
    w = w_ref[...]
    wh, wm, wl = _split3(w)
    logits = (_dot(xh, wh) + (_dot(xh, wm) + _dot(xm, wh)) + (_dot(xm, wm) + _dot(xl, wh) + _dot(xh, wl))
              + b_ref[...])
    lane = lax.broadcasted_iota(I32, (tm, LANES), 1)
    logits = jnp.where(lane < N_EXPERTS, logits, -jnp.inf)
    v1 = jnp.max(logits, axis=1, keepdims=True)
    e1 = jnp.min(jnp.where(logits == v1, lane, LANES), axis=1, keepdims=True)
    rest = jnp.where(lane == e1, -jnp.inf, logits)
    v2 = jnp.max(rest, axis=1, keepdims=True)
    e2 = jnp.min(jnp.where(rest == v2, lane, LANES), axis=1, keepdims=True)
    ex = jnp.exp(v2 - v1)
    g1 = 1.0 / (1.0 + ex)
    g2 = ex / (1.0 + ex)
    onehot = jnp.where((lane == e1) | (lane == e2), 1.0, 0.0)
    row = lax.broadcasted_iota(I32, (tm, tm), 0)
    col = lax.broadcasted_iota(I32, (tm, tm), 1)
    before = jnp.where(col < row, 1.0, 0.0).astype(BF16)
    rank = _dot(before, onehot.astype(BF16)) + carry_ref[...]
    r1 = jnp.sum(jnp.where(lane == e1, rank, 0.0), axis=1, keepdims=True)
    r2 = jnp.sum(jnp.where(lane == e2, rank, 0.0), axis=1, keepdims=True)
    carry_ref[...] += jnp.sum(onehot, axis=0, keepdims=True)
    cnt_ref[...] = carry_ref[...]
    info = jnp.where(lane == 0, e1.astype(F32), 0.0)
    info = jnp.where(lane == 1, e2.astype(F32), info)
    info = jnp.where(lane == 2, r1, info)
    info = jnp.where(lane == 3, r2, info)
    info = jnp.where(lane == 4, g1, info)
    info = jnp.where(lane == 5, g2, info)
    info_ref[...] = info


def _router(h, router_w, router_b, tm):
    n, d = h.shape
    wpad = jnp.zeros((d, LANES), F32).at[:, :N_EXPERTS].set(router_w.astype(F32))
    bpad = jnp.zeros((1, LANES), F32).at[0, :N_EXPERTS].set(router_b.astype(F32))
    return pl.pallas_call(
        _router_kernel,
        grid=(n // tm,),
        in_specs=[pl.BlockSpec((tm, d), lambda i: (i, 0)),
                  pl.BlockSpec((d, LANES), lambda i: (0, 0)),
                  pl.BlockSpec((1, LANES), lambda i: (0, 0))],
        out_specs=[pl.BlockSpec((tm, LANES), lambda i: (i, 0)), pl.BlockSpec((1, LANES), lambda i: (0, 0))],
        out_shape=[jax.ShapeDtypeStruct((n, LANES), F32), jax.ShapeDtypeStruct((1, LANES), F32)],
        scratch_shapes=[pltpu.VMEM((1, LANES), F32)],
        compiler_params=_params(("arbitrary",)),
        name="router",
    )(h, wpad, bpad)


def _expert_kernel(be_ref, tok_ref, x_ref, w1_ref, w3_ref, w2_ref, y_ref, rows_ref, xb_ref, acc_ref, sem, *, nf, nblk):
    del be_ref
    i = pl.program_id(0)
    f = pl.program_id(1)

    def copy(blk, slot, r):
        src = x_ref.at[pl.ds(tok_ref[blk * MOE_BLK + r], 1)]
        return pltpu.make_async_copy(src, rows_ref.at[slot, pl.ds(r, 1)], sem.at[slot])

    def gather(blk, slot):
        def body(r, carry):
            copy(blk, slot, r).start()
            return carry
        lax.fori_loop(0, MOE_BLK, body, 0)

    def drain(slot):
        def body(r, carry):
            copy(0, slot, 0).wait()
            return carry
        lax.fori_loop(0, MOE_BLK, body, 0)

    @pl.when(f == 0)
    def _():
        slot = i % 2

        @pl.when(i == 0)
        def _():
            gather(0, 0)

        drain(slot)

        @pl.when(i + 1 < nblk)
        def _():
            gather(i + 1, 1 - slot)

        xb_ref[...] = rows_ref[slot].astype(BF16)
        acc_ref[...] = jnp.zeros_like(acc_ref)

    x = xb_ref[...]
    a = _dot(x, w1_ref[...])
    c = _dot(x, w3_ref[...])
    mid = (a * jax.nn.sigmoid(a) * c).astype(BF16)
    acc_ref[...] += _dot(mid, w2_ref[...])

    @pl.when(f == nf - 1)
    def _():
        y_ref[...] = acc_ref[...]


def _experts(x, slot_tok, blk_expert, w1, w3, w2, tf):
    d = x.shape[1]
    cap = slot_tok.shape[0]
    nblk = cap // MOE_BLK
    nf = w1.shape[2] // tf
    grid_spec = pltpu.PrefetchScalarGridSpec(
        num_scalar_prefetch=2,
        grid=(nblk, nf),
        in_specs=[pl.BlockSpec(memory_space=pl.ANY),
                  pl.BlockSpec((None, d, tf), lambda i, f, be, tk: (be[i], 0, f)),
                  pl.BlockSpec((None, d, tf), lambda i, f, be, tk: (be[i], 0, f)),
                  pl.BlockSpec((None, tf, d), lambda i, f, be, tk: (be[i], f, 0))],
        out_specs=pl.BlockSpec((MOE_BLK, d), lambda i, f, be, tk: (i, 0)),
        scratch_shapes=[pltpu.VMEM((2, MOE_BLK, d), F32), pltpu.VMEM((MOE_BLK, d), BF16),
                        pltpu.VMEM((MOE_BLK, d), F32), pltpu.SemaphoreType.DMA((2,))])
    return pl.pallas_call(
        functools.partial(_expert_kernel, nf=nf, nblk=nblk),
        grid_spec=grid_spec,
        out_shape=jax.ShapeDtypeStruct((cap, d), F32),
        compiler_params=_params(("arbitrary", "arbitrary")),
        name="moe_experts",
    )(blk_expert, slot_tok, x, w1.astype(BF16), w3.astype(BF16), w2.astype(BF16))


def _combine_ln_kernel(d1_ref, d2_ref, yb_ref, h_ref, info_ref, g_ref, b_ref, o_ref, y1_ref, y2_ref, sem):
    tm = h_ref.shape[0]
    base = pl.program_id(0) * tm

    def copy(dest, buf, r):
        return pltpu.make_async_copy(yb_ref.at[pl.ds(dest, 1)], buf.at[pl.ds(r, 1)], sem)

    def issue(r, carry):
        copy(d1_ref[base + r], y1_ref, r).start()
        copy(d2_ref[base + r], y2_ref, r).start()
        return carry

    def drain(r, carry):
        copy(0, y1_ref, 0).wait()
        copy(0, y2_ref, 0).wait()
        return carry

    lax.fori_loop(0, tm, issue, 0)
    lax.fori_loop(0, tm, drain, 0)
    info = info_ref[...]
    moe = info[:, 4:5] * y1_ref[...] + info[:, 5:6] * y2_ref[...]
    o_ref[...] = _layer_norm(ALPHA * h_ref[...] + moe, g_ref[...], b_ref[...])


def _combine_ln(yb, d1, d2, h, info, g, b, tm):
    n, d = h.shape
    grid_spec = pltpu.PrefetchScalarGridSpec(
        num_scalar_prefetch=2,
        grid=(n // tm,),
        in_specs=[pl.BlockSpec(memory_space=pl.ANY),
                  pl.BlockSpec((tm, d), lambda i, a, c: (i, 0)),
                  pl.BlockSpec((tm, LANES), lambda i, a, c: (i, 0)),
                  pl.BlockSpec((1, d), lambda i, a, c: (0, 0)),
                  pl.BlockSpec((1, d), lambda i, a, c: (0, 0))],
        out_specs=pl.BlockSpec((tm, d), lambda i, a, c: (i, 0)),
        scratch_shapes=[pltpu.VMEM((tm, d), F32), pltpu.VMEM((tm, d), F32), pltpu.SemaphoreType.DMA(())])
    return pl.pallas_call(
        _combine_ln_kernel,
        grid_spec=grid_spec,
        out_shape=jax.ShapeDtypeStruct((n, d), F32),
        compiler_params=_params(("arbitrary",)),
        name="moe_combine_ln",
    )(d1, d2, yb, h, info, g.reshape(1, d), b.reshape(1, d))


def _moe_ln(h, router_w, router_b, w1, w3, w2, g, b, tm):
    n, d = h.shape
    info, counts = _router(h, router_w, router_b, tm)
    counts = counts[0, :N_EXPERTS].astype(I32)
    padded = (counts + MOE_BLK - 1) // MOE_BLK * MOE_BLK
    pad_end = jnp.cumsum(padded)
    pad_start = pad_end - padded
    e1, e2 = info[:, 0].astype(I32), info[:, 1].astype(I32)
    d1 = pad_start[e1] + info[:, 2].astype(I32)
    d2 = pad_start[e2] + info[:, 3].astype(I32)
    nblk = -(-(n * TOP_K) // MOE_BLK) + N_EXPERTS
    blk_expert = jnp.minimum(jnp.searchsorted(pad_end, jnp.arange(nblk, dtype=I32) * MOE_BLK, side='right'),
                             N_EXPERTS - 1).astype(I32)
    tok = jnp.arange(n, dtype=I32)
    slot_tok = jnp.zeros((nblk * MOE_BLK,), I32).at[jnp.concatenate([d1, d2])].set(
        jnp.concatenate([tok, tok]), unique_indices=True)
    yb = _experts(h, slot_tok, blk_expert, w1, w3, w2, 512)
    return _combine_ln(yb, d1, d2, h, info, g, b, tm)


def kernel(x, meta_tokens, w_in_even, conv_w, s5_a_re, s5_a_im, s5_log_dt, s5_b_re, s5_b_im, s5_c_re, s5_c_im, s5_d, s5_w_glu, w_out_even, ln_mix_even_g, ln_mix_even_b, ffn_w1, ffn_w3, ffn_w2, ln_ffn_even_g, ln_ffn_even_b, w_in_odd, mlstm_b_i, mlstm_b_f, rel_bias_table, w_out_odd, ln_mix_odd_g, ln_mix_odd_b, router_w, router_b, expert_w1, expert_w3, expert_w2, ln_ffn_odd_g, ln_ffn_odd_b):
    bsz, seq, d = x.shape
    topk = min(TOPK_MAX, seq // 4)
    lh = seq + N_META
    lp = -(-lh // SEQ_TILE) * SEQ_TILE
    n = bsz * lp
    tm = _pick(n, (1280, 640, 512, 256))
    tln = _pick(n, (512, 256))
    tseq = _pick(lp, (1280, 640, 256))

    h = jnp.concatenate([jnp.broadcast_to(meta_tokens.astype(x.dtype)[None], (bsz, N_META, d)), x,
                         jnp.zeros((bsz, lp - lh, d), x.dtype)], axis=1).reshape(n, d)
    hb = h.astype(BF16)

    p = _proj(hb, w_in_even[0].astype(BF16), F32, tm, 1280)
    y_conv = _conv_mixer(p, conv_w[0].astype(F32), lp, tseq, 512)
    ops = _s5_operators(s5_a_re[0], s5_a_im[0], s5_log_dt[0], s5_b_re[0], s5_b_im[0], s5_c_re[0], s5_c_im[0], S5_T)
    u = p[:, 3 * D_CONV:].astype(BF16)
    y_lin = _s5_linear(u, ops, bsz, lp)
    y_s5 = _s5_glu(y_lin, p, s5_d[0], s5_w_glu[0], tm)
    h, hb = _outproj_ln(y_conv, y_s5, w_out_even[0], h, ln_mix_even_g[0], ln_mix_even_b[0], tln)
    h, hb = _ffn_ln(h, ffn_w1[0], ffn_w3[0], ffn_w2[0], ln_ffn_even_g[0], ln_ffn_even_b[0], tln, 512)

    w = w_in_odd[0]
    o = np.cumsum([0] + [D_MLSTM] * 4 + [MLSTM_HEADS] * 2 + [D_DSA] * 3 + [IDX_HEADS * IDX_DH, IDX_DH, IDX_HEADS])
    seg = lambda j: w[:, o[j]:o[j + 1]]
    w_big = jnp.concatenate([seg(0), seg(1), seg(2), seg(3), seg(6), seg(7), seg(8), seg(9)], axis=1)
    w_small = jnp.concatenate([seg(4), seg(5), seg(10), seg(11)], axis=1)
    w_small = jnp.pad(w_small, ((0, 0), (0, LANES - w_small.shape[1])))
    big = _proj(hb, w_big.astype(BF16), BF16, tm, 1280)
    small = _proj(hb, w_small.astype(BF16), F32, tm, LANES)
    gate_bias = jnp.zeros((1, LANES), F32).at[0, :MLSTM_HEADS].set(mlstm_b_i[0].astype(F32))
    gate_bias = gate_bias.at[0, MLSTM_HEADS:2 * MLSTM_HEADS].set(mlstm_b_f[0].astype(F32))
    y_m = _mlstm(big, small, gate_bias, bsz, lp)
    qi = big[:, 7 * D_DSA:].reshape(bsz, lp, IDX_HEADS, IDX_DH).transpose(0, 2, 1, 3)
    kidx = small[:, 2 * MLSTM_HEADS:2 * MLSTM_HEADS + IDX_DH].astype(BF16).reshape(bsz, lp, IDX_DH)
    widx = small[:, 2 * MLSTM_HEADS + IDX_DH:2 * MLSTM_HEADS + IDX_DH + IDX_HEADS]
    mask = _dsa_select(qi, kidx, widx, bsz, lp, topk)
    y_d = _dsa_attention(big, mask, rel_bias_table, bsz, lp)
    h, hb = _outproj_ln(y_m, y_d, w_out_odd[0], h, ln_mix_odd_g[0], ln_mix_odd_b[0], tln)
    h = _moe_ln(h, router_w[0], router_b[0], expert_w1[0], expert_w3[0], expert_w2[0],
                ln_ffn_odd_g[0], ln_ffn_odd_b[0], tln)
    return h.reshape(bsz, lp, d)[:, N_META:lh]
```

```python
import functools
import math

import jax
import jax.numpy as jnp
import numpy as np
from jax import lax
from jax.experimental import pallas as pl
from jax.experimental.pallas import tpu as pltpu

F32 = jnp.float32
BF16 = jnp.bfloat16
I32 = jnp.int32
I16 = jnp.int16

D_MODEL = 2048
DEPTH = 2
CHUNK = 64
N_META = 16
CONV_WIDTH = 3
D_CONV = 1536
S5_CH = 16
S5_STATE = 64
D_SSM = 512
S5_GROUPS = D_SSM // S5_CH
MLSTM_HEADS = 4
MLSTM_DH = 256
D_MLSTM = MLSTM_HEADS * MLSTM_DH
DSA_HEADS = 8
DSA_DH = 128
D_DSA = DSA_HEADS * DSA_DH
IDX_HEADS = 8
IDX_DH = 64
TOPK_MAX = 256
NUM_BUCKETS = 32
MAX_DISTANCE = 128
D_FF = 5632
N_EXPERTS = 8
TOP_K = 2
D_EXPERT = 5632
ALPHA = (2 * DEPTH) ** 0.25
LN_EPS = 1e-5

LANES = 128
SEQ_TILE = 256
S5_T = 64
SEARCH_W = 1024
MOE_BLK = 512
VMEM_LIMIT = 56 * 1024 * 1024
NEG = -1e30
LOG2E = math.log2(math.e)
INT_MIN = -(2 ** 31)


def _params(sem, vmem=VMEM_LIMIT):
    return pltpu.CompilerParams(dimension_semantics=sem, vmem_limit_bytes=vmem)


def _pick(n, candidates):
    for c in candidates:
        if n % c == 0:
            return c
    raise ValueError(f"no tile in {candidates} divides {n}")


def _dot(a, b):
    return jnp.dot(a, b, preferred_element_type=F32)


def _dot_nt(a, b):
    return lax.dot_general(a, b, (((1,), (1,)), ((), ())), preferred_element_type=F32)


def _split3(x):
    hi = x.astype(BF16)
    r = x - hi.astype(F32)
    mid = r.astype(BF16)
    lo = (r - mid.astype(F32)).astype(BF16)
    return hi, mid, lo


def _layer_norm(z, g, b):
    mu = jnp.mean(z, axis=-1, keepdims=True)
    zc = z - mu
    var = jnp.mean(zc * zc, axis=-1, keepdims=True)
    return zc * lax.rsqrt(var + LN_EPS) * g + b


def _proj_kernel(x_ref, w_ref, o_ref):
    o_ref[...] = _dot(x_ref[...], w_ref[...]).astype(o_ref.dtype)


def _proj(x, w, out_dtype, tm, tn):
    n, k = x.shape
    m = w.shape[1]
    return pl.pallas_call(
        _proj_kernel,
        grid=(n // tm, m // tn),
        in_specs=[pl.BlockSpec((tm, k), lambda i, j: (i, 0)),
                  pl.BlockSpec((k, tn), lambda i, j: (0, j))],
        out_specs=pl.BlockSpec((tm, tn), lambda i, j: (i, j)),
        out_shape=jax.ShapeDtypeStruct((n, m), out_dtype),
        compiler_params=_params(("parallel", "parallel")),
        name="proj",
    )(x, w)


def _outproj_ln_kernel(ya_ref, yb_ref, wa_ref, wb_ref, h_ref, g_ref, b_ref, o_ref, ob_ref):
    mix = _dot(ya_ref[...], wa_ref[...]) + _dot(yb_ref[...], wb_ref[...])
    out = _layer_norm(ALPHA * h_ref[...] + mix, g_ref[...], b_ref[...])
    o_ref[...] = out
    ob_ref[...] = out.astype(BF16)


def _outproj_ln(ya, yb, w, h, g, b, tm):
    n, d = h.shape
    ka, kb = ya.shape[1], yb.shape[1]
    wa, wb = w[:ka].astype(BF16), w[ka:].astype(BF16)
    row = lambda i: (i, 0)
    fixed = lambda i: (0, 0)
    return pl.pallas_call(
        _outproj_ln_kernel,
        grid=(n // tm,),
        in_specs=[pl.BlockSpec((tm, ka), row), pl.BlockSpec((tm, kb), row),
                  pl.BlockSpec((ka, d), fixed), pl.BlockSpec((kb, d), fixed),
                  pl.BlockSpec((tm, d), row),
                  pl.BlockSpec((1, d), fixed), pl.BlockSpec((1, d), fixed)],
        out_specs=[pl.BlockSpec((tm, d), row), pl.BlockSpec((tm, d), row)],
        out_shape=[jax.ShapeDtypeStruct((n, d), F32), jax.ShapeDtypeStruct((n, d), BF16)],
        compiler_params=_params(("parallel",)),
        name="outproj_ln",
    )(ya, yb, wa, wb, h, g.reshape(1, d), b.reshape(1, d))


def _conv_kernel(b_ref, c_ref, x_ref, ch_ref, xh_ref, w_ref, y_ref, *, tiles_per_seq):
    i = pl.program_id(0)
    z = c_ref[...] * x_ref[...]
    zh = ch_ref[...] * xh_ref[...]
    zh = jnp.where(i % tiles_per_seq == 0, 0.0, zh)
    r = lax.broadcasted_iota(I32, z.shape, 0)
    z1 = jnp.where(r == 0, zh[7:8], pltpu.roll(z, 1, 0))
    z2 = jnp.where(r == 0, zh[6:7], jnp.where(r == 1, zh[7:8], pltpu.roll(z, 2, 0)))
    w = w_ref[...]
    conv = w[0:1] * z2 + w[1:2] * z1 + w[2:3] * z
    y_ref[...] = (b_ref[...] * conv).astype(y_ref.dtype)


def _conv_mixer(p, conv_w, lp, tm, tc):
    n = p.shape[0]
    ncb = D_CONV // tc
    hb = tm // 8
    body = functools.partial(_conv_kernel, tiles_per_seq=lp // tm)
    return pl.pallas_call(
        body,
        grid=(n // tm, ncb),
        in_specs=[pl.BlockSpec((tm, tc), lambda i, j: (i, j)),
                  pl.BlockSpec((tm, tc), lambda i, j: (i, ncb + j)),
                  pl.BlockSpec((tm, tc), lambda i, j: (i, 2 * ncb + j)),
                  pl.BlockSpec((8, tc), lambda i, j: (jnp.maximum(i * hb - 1, 0), ncb + j)),
                  pl.BlockSpec((8, tc), lambda i, j: (jnp.maximum(i * hb - 1, 0), 2 * ncb + j)),
                  pl.BlockSpec((CONV_WIDTH, tc), lambda i, j: (0, j))],
        out_specs=pl.BlockSpec((tm, tc), lambda i, j: (i, j)),
        out_shape=jax.ShapeDtypeStruct((n, D_CONV), BF16),
        compiler_params=_params(("parallel", "parallel")),
        name="conv_mixer",
    )(p, p, p, p, p, conv_w)


def _s5_operators(a_re, a_im, log_dt, b_re, b_im, c_re, c_im, t):
    a = lax.complex(a_re.astype(F32), a_im.astype(F32))
    dt = jnp.exp(log_dt.astype(F32))[:, None]
    a_bar = jnp.exp(dt * a)
    b_bar = ((a_bar - 1.0) / a)[:, :, None] * lax.complex(b_re.astype(F32), b_im.astype(F32))
    c = lax.complex(c_re.astype(F32), c_im.astype(F32))
    lag = jnp.arange(t + 1, dtype=F32)[:, None, None]
    pw = jnp.exp(lag * (dt * a)[None])
    kern = jnp.real(jnp.einsum('gcp,lgp,gpd->glcd', c, pw[:t], b_bar))
    lagged = jnp.concatenate([jnp.zeros_like(kern), kern], axis=1)
    blocks = jnp.stack([lagged[:, t - j:2 * t - j] for j in range(t)], axis=1)
    g = a.shape[0]
    toep = blocks.transpose(0, 1, 4, 2, 3).reshape(g, t * S5_CH, t * S5_CH)
    wc = pw[:t][::-1].transpose(1, 0, 2)[:, :, None, :] * b_bar.transpose(0, 2, 1)[:, None, :, :]
    wc = wc.reshape(g, t * S5_CH, S5_STATE)
    w_in = jnp.concatenate([jnp.real(wc), jnp.imag(wc)], axis=-1)
    cp = c.transpose(0, 2, 1)[:, :, None, :] * pw[1:t + 1].transpose(1, 2, 0)[:, :, :, None]
    cp = cp.reshape(g, S5_STATE, t * S5_CH)
    v_out = jnp.concatenate([jnp.real(cp), -jnp.imag(cp)], axis=1)
    a_t = jnp.stack([jnp.real(pw[t]), jnp.imag(pw[t])], axis=1)
    return toep.astype(BF16), w_in.astype(BF16), v_out.astype(BF16), a_t


def _s5_kernel(u_ref, toep_ref, win_ref, vout_ref, at_ref, y_ref, er_ref, ei_ref, hr_ref, hi_ref, *, nbatch, nc):
    u = u_ref[...]
    e = _dot(u, win_ref[...])
    er_ref[...] = e[:, :S5_STATE]
    ei_ref[...] = e[:, S5_STATE:]
    at = at_ref[...]
    ar, ai = at[0:1], at[1:2]

    def step(c, carry):
        out = []
        for b in range(nbatch):
            hr, hi = carry[2 * b], carry[2 * b + 1]
            row = b * nc + c
            hr_ref[pl.ds(row, 1), :] = hr
            hi_ref[pl.ds(row, 1), :] = hi
            er = er_ref[pl.ds(row, 1), :]
            ei = ei_ref[pl.ds(row, 1), :]
            out += [ar * hr - ai * hi + er, ar * hi + ai * hr + ei]
        return tuple(out)

    zero = jnp.zeros((1, S5_STATE), F32)
    lax.fori_loop(0, nc, step, (zero,) * (2 * nbatch))
    vout = vout_ref[...]
    y = _dot(u, toep_ref[...])
    for part, rows in ((hr_ref[...], vout[:S5_STATE]), (hi_ref[...], vout[S5_STATE:])):
        hi_, mid_, _ = _split3(part)
        y += _dot(hi_, rows) + _dot(mid_, rows)
    y_ref[...] = y


def _s5_linear(u, ops, nbatch, lp):
    toep, w_in, v_out, a_t = ops
    n = u.shape[0]
    t = S5_T
    nc = lp // t
    rows = nbatch * nc
    width = t * S5_CH
    ug = u.reshape(rows, t, S5_GROUPS, S5_CH).transpose(2, 0, 1, 3).reshape(S5_GROUPS, rows, width)
    grp = lambda g: (g, 0, 0)
    yg = pl.pallas_call(
        functools.partial(_s5_kernel, nbatch=nbatch, nc=nc),
        grid=(S5_GROUPS,),
        in_specs=[pl.BlockSpec((None, rows, width), grp),
                  pl.BlockSpec((None, width, width), grp),
                  pl.BlockSpec((None, width, 2 * S5_STATE), grp),
                  pl.BlockSpec((None, 2 * S5_STATE, width), grp),
                  pl.BlockSpec((None, 2, S5_STATE), grp)],
        out_specs=pl.BlockSpec((None, rows, width), grp),
        out_shape=jax.ShapeDtypeStruct((S5_GROUPS, rows, width), F32),
        scratch_shapes=[pltpu.VMEM((rows, S5_STATE), F32)] * 4,
        compiler_params=_params(("parallel",)),
        name="s5_linear",
    )(ug, toep, w_in, v_out, a_t)
    return yg.reshape(S5_GROUPS, rows, t, S5_CH).transpose(1, 2, 0, 3).reshape(n, D_SSM)


def _s5_glu_kernel(y_ref, u_ref, d_ref, w_ref, o_ref):
    y = y_ref[...] + d_ref[...] * u_ref[...]
    y = jax.nn.gelu(y)
    hi, mid, _ = _split3(y)
    w = w_ref[...]
    gate = _dot(hi, w) + _dot(mid, w)
    o_ref[...] = (y * jax.nn.sigmoid(gate)).astype(o_ref.dtype)


def _s5_glu(y_lin, p, d_skip, w_glu, tm):
    n = y_lin.shape[0]
    ucol = (3 * D_CONV) // D_SSM
    wbd = jnp.einsum('gcd,gh->gdhc', w_glu.astype(F32), jnp.eye(S5_GROUPS, dtype=F32)).reshape(D_SSM, D_SSM)
    return pl.pallas_call(
        _s5_glu_kernel,
        grid=(n // tm,),
        in_specs=[pl.BlockSpec((tm, D_SSM), lambda i: (i, 0)),
                  pl.BlockSpec((tm, D_SSM), lambda i: (i, ucol)),
                  pl.BlockSpec((1, D_SSM), lambda i: (0, 0)),
                  pl.BlockSpec((D_SSM, D_SSM), lambda i: (0, 0))],
        out_specs=pl.BlockSpec((tm, D_SSM), lambda i: (i, 0)),
        out_shape=jax.ShapeDtypeStruct((n, D_SSM), BF16),
        compiler_params=_params(("parallel",)),
        name="s5_glu",
    )(y_lin, p, d_skip.reshape(1, D_SSM).astype(F32), wbd.astype(BF16))


def _ffn_ln_kernel(h_ref, w1_ref, w3_ref, w2_ref, g_ref, b_ref, o_ref, ob_ref, xb_ref, acc_ref, *, nf):
    f = pl.program_id(1)

    @pl.when(f == 0)
    def _():
        xb_ref[...] = h_ref[...].astype(BF16)
        acc_ref[...] = jnp.zeros_like(acc_ref)

    x = xb_ref[...]
    a = _dot(x, w1_ref[...])
    c = _dot(x, w3_ref[...])
    mid = (a * jax.nn.sigmoid(a) * c).astype(BF16)
    acc_ref[...] += _dot(mid, w2_ref[...])

    @pl.when(f == nf - 1)
    def _():
        out = _layer_norm(ALPHA * h_ref[...] + acc_ref[...], g_ref[...], b_ref[...])
        o_ref[...] = out
        ob_ref[...] = out.astype(BF16)


def _ffn_ln(h, w1, w3, w2, g, b, tm, tf):
    n, d = h.shape
    nf = w1.shape[1] // tf
    return pl.pallas_call(
        functools.partial(_ffn_ln_kernel, nf=nf),
        grid=(n // tm, nf),
        in_specs=[pl.BlockSpec((tm, d), lambda i, f: (i, 0)),
                  pl.BlockSpec((d, tf), lambda i, f: (0, f)),
                  pl.BlockSpec((d, tf), lambda i, f: (0, f)),
                  pl.BlockSpec((tf, d), lambda i, f: (f, 0)),
                  pl.BlockSpec((1, d), lambda i, f: (0, 0)),
                  pl.BlockSpec((1, d), lambda i, f: (0, 0))],
        out_specs=[pl.BlockSpec((tm, d), lambda i, f: (i, 0)), pl.BlockSpec((tm, d), lambda i, f: (i, 0))],
        out_shape=[jax.ShapeDtypeStruct((n, d), F32), jax.ShapeDtypeStruct((n, d), BF16)],
        scratch_shapes=[pltpu.VMEM((tm, d), BF16), pltpu.VMEM((tm, d), F32)],
        compiler_params=_params(("parallel", "arbitrary")),
        name="ffn_ln",
    )(h, w1.astype(BF16), w3.astype(BF16), w2.astype(BF16), g.reshape(1, d), b.reshape(1, d))


def _log_sigmoid(x):
    return jnp.minimum(x, 0.0) - jnp.log(1.0 + jnp.exp(-jnp.abs(x)))


def _mlstm_kernel(q_ref, k_ref, v_ref, o_ref, g_ref, gb_ref, y_ref, ct_ref, n_ref, m_ref):
    c = pl.program_id(1)
    t = q_ref.shape[0]
    dh = MLSTM_DH

    @pl.when(c == 0)
    def _():
        ct_ref[...] = jnp.zeros_like(ct_ref)
        n_ref[...] = jnp.zeros_like(n_ref)
        m_ref[...] = jnp.zeros_like(m_ref)

    gates = g_ref[...] + gb_ref[...]
    lf = _log_sigmoid(gates)
    row = lax.broadcasted_iota(I32, (t, t), 0)
    col = lax.broadcasted_iota(I32, (t, t), 1)
    tril = col <= row
    tril_b = jnp.where(tril, 1.0, 0.0).astype(BF16)
    hi, mid, lo = _split3(lf)
    bcum = _dot(tril_b, hi) + _dot(tril_b, mid) + _dot(tril_b, lo)
    gates_t = gates.T
    bcum_t = bcum.T
    for h in range(MLSTM_HEADS):
        sl = slice(h * dh, (h + 1) * dh)
        fh = MLSTM_HEADS + h
        li_row, li_col = gates_t[h:h + 1, :], gates[:, h:h + 1]
        bc_row, bc_col = bcum_t[fh:fh + 1, :], bcum[:, fh:fh + 1]
        m_prev = m_ref[h:h + 1, 0:1]
        dmat = jnp.where(tril, bc_col - bc_row + li_row, -jnp.inf)
        m_inter = bc_col + m_prev
        m_t = jnp.maximum(m_inter, jnp.max(dmat, axis=1, keepdims=True))
        q = q_ref[:, sl]
        k = k_ref[:, sl] * (dh ** -0.5)
        v = v_ref[:, sl]
        s = _dot_nt(q, k) * jnp.exp(dmat - m_t)
        w_inter = jnp.exp(m_inter - m_t)
        ct = ct_ref[h]
        nvec = n_ref[h:h + 1, :]
        num = _dot(s.astype(BF16), v) + w_inter * _dot(q, ct.astype(BF16))
        den = jnp.sum(s, axis=1, keepdims=True) + w_inter * jnp.sum(q.astype(F32) * nvec, axis=1, keepdims=True)
        h_out = num / jnp.maximum(jnp.abs(den), jnp.exp(-m_t))
        y_ref[:, sl] = (jax.nn.sigmoid(o_ref[:, sl].astype(F32)) * h_out).astype(y_ref.dtype)
        b_last = bc_col[t - 1:t, :]
        m_new = jnp.maximum(b_last + m_prev, jnp.max(b_last - bc_row + li_row, axis=1, keepdims=True))
        decay = jnp.exp(b_last + m_prev - m_new)
        kw = k.astype(F32) * jnp.exp(b_last - bc_col + li_col - m_new)
        ct_ref[h] = decay * ct + _dot(kw.T.astype(BF16), v)
        n_ref[h:h + 1, :] = decay * nvec + jnp.sum(kw, axis=0, keepdims=True)
        m_ref[h:h + 1, :] = jnp.broadcast_to(m_new, (1, LANES))


def _mlstm(big, small, gate_bias, nbatch, lp):
    n = big.shape[0]
    t = SEQ_TILE
    nc = lp // t
    blk = lambda j: pl.BlockSpec((t, D_MLSTM), lambda b, c: (b * nc + c, j))
    return pl.pallas_call(
        _mlstm_kernel,
        grid=(nbatch, nc),
        in_specs=[blk(0), blk(1), blk(2), blk(3),
                  pl.BlockSpec((t, LANES), lambda b, c: (b * nc + c, 0)),
                  pl.BlockSpec((1, LANES), lambda b, c: (0, 0))],
        out_specs=pl.BlockSpec((t, D_MLSTM), lambda b, c: (b * nc + c, 0)),
        out_shape=jax.ShapeDtypeStruct((n, D_MLSTM), BF16),
        scratch_shapes=[pltpu.VMEM((MLSTM_HEADS, MLSTM_DH, MLSTM_DH), F32),
                        pltpu.VMEM((8, MLSTM_DH), F32),
                        pltpu.VMEM((8, LANES), F32)],
        compiler_params=_params(("parallel", "arbitrary")),
        name="mlstm",
    )(big, big, big, big, small, gate_bias)


def _chunk_end(pos):
    return jnp.where(pos < N_META, N_META, N_META + CHUNK + CHUNK * ((pos - N_META) >> 6))


def _t5_bucket(rel):
    half = NUM_BUCKETS // 2
    max_exact = half // 2
    ret = jnp.where(rel > 0, half, 0)
    n = jnp.abs(rel)
    nf = jnp.maximum(n, 1).astype(F32)
    large = max_exact + (jnp.log(nf / max_exact) / math.log(MAX_DISTANCE / max_exact) * (half - max_exact)).astype(I32)
    large = jnp.minimum(large, half - 1)
    return ret + jnp.where(n < max_exact, n, large)


def _f32_order_key(x):
    bits = lax.bitcast_convert_type(x, I32)
    return bits ^ ((bits >> 31) & 0x7FFFFFFF)


def _lanes(x, width):
    return jnp.concatenate([x] * (width // LANES), axis=1)


def _dsa_select_kernel(qi_ref, kidx_ref, w_ref, mask_ref, hi_ref, lo_ref, wb_ref, lim_ref, *, lp, topk):
    qb, sw = SEQ_TILE, SEARCH_W
    b = pl.program_id(1)
    nkt = jnp.minimum(b + 2, lp // qb)
    nst = (nkt + sw // qb - 1) // (sw // qb)
    wi = w_ref[...] * (IDX_HEADS ** -0.5) * (IDX_DH ** -0.5)
    for h in range(IDX_HEADS):
        wb_ref[h] = jnp.broadcast_to(wi[:, h:h + 1], (qb, LANES))
    end = _chunk_end(b * qb + lax.broadcasted_iota(I32, (qb, 1), 0))
    lane = lax.broadcasted_iota(I32, (qb, qb), 1)
    low16 = jnp.int16(-(2 ** 15))

    def store_keys(off, key):
        hi_ref[:, pl.ds(off, qb)] = (key >> 16).astype(I16)
        lo_ref[:, pl.ds(off, qb)] = key.astype(I16) ^ low16

    def fill(kt, visible_only):
        off = pl.multiple_of(kt * qb, qb)
        kid = kidx_ref[pl.ds(off, qb), :]
        score = None
        for h in range(IDX_HEADS):
            term = _lanes(wb_ref[h], qb) * jnp.maximum(_dot_nt(qi_ref[h], kid), 0.0)
            score = term if score is None else score + term
        key = _f32_order_key(score)
        if visible_only:
            key = jnp.where(off + lane < end, key, INT_MIN)
        store_keys(off, key)

    def loop(lo, hi, body):
        lax.fori_loop(lo, hi, lambda i, c: (body(i), c)[1], 0)

    loop(0, jnp.minimum(b, nkt), lambda kt: fill(kt, False))
    loop(jnp.minimum(b, nkt), nkt, lambda kt: fill(kt, True))
    loop(nkt, nst * (sw // qb), lambda kt: store_keys(pl.multiple_of(kt * qb, qb), jnp.full((qb, qb), INT_MIN, I32)))

    one = jnp.ones((qb, sw), BF16)
    zero = jnp.zeros((qb, sw), BF16)
    ones_rhs = jnp.ones((LANES, LANES), BF16)

    def count(*preds):
        def body(st, accs):
            off = pl.multiple_of(st * sw, sw)
            hi, lo = hi_ref[:, pl.ds(off, sw)], lo_ref[:, pl.ds(off, sw)]
            out = []
            for pred, acc in zip(preds, accs):
                hit = pred(hi, lo, off)
                parts = [hit[:, j * LANES:(j + 1) * LANES] for j in range(sw // LANES)]
                while len(parts) > 1:
                    parts = [parts[j] + parts[j + 1] for j in range(0, len(parts), 2)]
                out.append(acc + parts[0])
            return tuple(out)
        accs = lax.fori_loop(0, nst, body, (jnp.zeros((qb, LANES), BF16),) * len(preds))
        return [_dot(acc, ones_rhs) for acc in accs]

    def as16(u):
        return _lanes((u - 2 ** 15).astype(I16), sw)

    def search(pred_ge, target):
        def rnd(i, u):
            cand = u | jnp.left_shift(jnp.int32(1), 15 - i)
            cnt, = count(pred_ge(as16(cand)))
            return jnp.where(cnt >= target, cand, u)
        return lax.fori_loop(0, 16, rnd, jnp.zeros((qb, LANES), I32))

    kf = jnp.float32(topk)
    t_hi = search(lambda c: lambda hi, lo, off: jnp.where(hi >= c, one, zero), kf)
    thi16 = as16(t_hi)
    above_hi, = count(lambda hi, lo, off: jnp.where(hi > thi16, one, zero))

    def restrict(st):
        off = pl.multiple_of(st * sw, sw)
        lo_ref[:, pl.ds(off, sw)] = jnp.where(hi_ref[:, pl.ds(off, sw)] == thi16, lo_ref[:, pl.ds(off, sw)], low16)

    loop(0, nst, restrict)
    t_lo = search(lambda c: lambda hi, lo, off: jnp.where(lo >= c, one, zero), kf - above_hi)
    tlo16 = as16(t_lo)
    above_lo, equal = count(lambda hi, lo, off: jnp.where(lo > tlo16, one, zero),
                            lambda hi, lo, off: jnp.where(hi == thi16, jnp.where(lo == tlo16, one, zero), zero))
    above = above_hi + above_lo
    lowest = (t_hi == 0) & (t_lo == 0)
    tie = (above + equal > kf) & jnp.logical_not(lowest)
    lim_ref[...] = jnp.full((qb, LANES), lp, I32)

    def col16(off, width):
        return (off + lax.broadcasted_iota(I32, (qb, width), 1)).astype(I16)

    @pl.when(jnp.max(jnp.where(tie, 1, 0)) > 0)
    def _():
        need = kf - above

        def rnd(i, x):
            cand = x | jnp.left_shift(jnp.int32(1), 14 - i)
            c16 = _lanes(cand.astype(I16), sw)
            cnt, = count(lambda hi, lo, off: jnp.where(
                hi == thi16, jnp.where(lo == tlo16, jnp.where(col16(off, sw) < c16, one, zero), zero), zero))
            return jnp.where(cnt < need, cand, x)

        x = lax.fori_loop(0, 15, rnd, jnp.zeros((qb, LANES), I32))
        lim_ref[...] = jnp.where(tie, x, lp)

    thi_t, tlo_t, lim_t = (_lanes(v.astype(I16), qb) for v in (t_hi - 2 ** 15, t_lo - 2 ** 15, lim_ref[...]))
    keep = jnp.zeros((qb, qb), BF16)
    drop = jnp.full((qb, qb), NEG, BF16)

    def emit(kt):
        off = pl.multiple_of(kt * qb, qb)
        hi = hi_ref[:, pl.ds(off, qb)]
        lo = lo_ref[:, pl.ds(off, qb)]
        at_thr = jnp.where(lo == tlo_t, jnp.where(col16(off, qb) <= lim_t, keep, drop), drop)
        in_lo = jnp.where(lo > tlo_t, keep, at_thr)
        mask_ref[:, pl.ds(off, qb)] = jnp.where(hi > thi_t, keep, jnp.where(hi == thi_t, in_lo, drop))

    def blank(kt):
        mask_ref[:, pl.ds(pl.multiple_of(kt * qb, qb), qb)] = drop

    loop(0, nkt, emit)
    loop(nkt, lp // qb, blank)


def _dsa_select(qi, kidx, w, nbatch, lp, topk):
    qb = SEQ_TILE
    nqb = lp // qb
    lpad = -(-lp // SEARCH_W) * SEARCH_W
    assert lpad <= 256 * LANES
    return pl.pallas_call(
        functools.partial(_dsa_select_kernel, lp=lp, topk=topk),
        grid=(nbatch, nqb),
        in_specs=[pl.BlockSpec((None, IDX_HEADS, qb, IDX_DH), lambda bb, b: (bb, 0, b, 0)),
                  pl.BlockSpec((None, lp, IDX_DH), lambda bb, b: (bb, 0, 0)),
                  pl.BlockSpec((qb, IDX_HEADS), lambda bb, b: (bb * nqb + b, 0))],
        out_specs=pl.BlockSpec((qb, lp), lambda bb, b: (bb * nqb + b, 0)),
        out_shape=jax.ShapeDtypeStruct((nbatch * lp, lp), BF16),
        scratch_shapes=[pltpu.VMEM((qb, lpad), I16), pltpu.VMEM((qb, lpad), I16),
                        pltpu.VMEM((IDX_HEADS, qb, LANES), F32), pltpu.VMEM((qb, LANES), I32)],
        compiler_params=_params(("parallel", "parallel")),
        name="dsa_select",
    )(qi, kidx, w)


def _dsa_attn_kernel(qb_ref, kt_ref, q_ref, k_ref, v_ref, mask_ref, band_ref, far_ref, y_ref,
                     s_ref, m_ref, l_ref, acc_ref, *, lp):
    t = SEQ_TILE
    step = pl.program_id(1)
    b = qb_ref[step]
    kt = kt_ref[step]
    last_kt = jnp.minimum(b + 1, lp // t - 1)

    @pl.when(kt == 0)
    def _():
        m_ref[...] = jnp.full_like(m_ref, NEG)
        l_ref[...] = jnp.zeros_like(l_ref)
        acc_ref[...] = jnp.zeros_like(acc_ref)

    heads = [slice(h * DSA_DH, (h + 1) * DSA_DH) for h in range(DSA_HEADS)]
    for h, sl in enumerate(heads):
        s_ref[h] = _dot_nt(q_ref[:, sl], k_ref[:, sl])
    mask = mask_ref[...].astype(F32)

    ones_v = jnp.ones((t, LANES), BF16)

    def update(z, shift):
        m_prev = m_ref[...]
        m_next = jnp.maximum(m_prev, jnp.max(z, axis=2, keepdims=True) + shift)
        p = jnp.exp2(z - jnp.concatenate([m_next - shift] * (t // LANES), axis=2)).astype(BF16)
        alpha = jnp.exp2(m_prev - m_next)
        m_ref[...] = m_next
        for h, sl in enumerate(heads):
            pv = _dot(p[h], jnp.concatenate([v_ref[:, sl], ones_v], axis=1))
            acc_ref[:, sl] = alpha[h] * acc_ref[:, sl] + pv[:, :DSA_DH]
            l_ref[h] = alpha[h] * l_ref[h] + pv[:, DSA_DH:]

    near = kt >= b - 1
    scale = DSA_DH ** -0.5 * LOG2E

    @pl.when(near)
    def _():
        bias = band_ref[kt - b + 1] + mask[None]
        update(s_ref[...] * scale + bias, jnp.zeros((DSA_HEADS, 1, LANES), F32))

    @pl.when(jnp.logical_not(near))
    def _():
        update(s_ref[...] * scale + mask[None], far_ref[...])

    @pl.when(kt == last_kt)
    def _():
        for h, sl in enumerate(heads):
            y_ref[:, sl] = (acc_ref[:, sl] / l_ref[h]).astype(y_ref.dtype)


def _dsa_bias_band(rel_bias_table):
    t = SEQ_TILE
    i = jnp.arange(t, dtype=I32)[:, None]
    c = jnp.arange(3 * t, dtype=I32)[None, :] - t
    bias = rel_bias_table.astype(F32)[_t5_bucket(c - i)]
    band = jnp.where((c < _chunk_end(i))[:, :, None], bias * LOG2E, NEG)
    band = band.reshape(t, 3, t, DSA_HEADS).transpose(1, 3, 0, 2)
    far = rel_bias_table.astype(F32)[_t5_bucket(jnp.int32(-(t + 1)))] * LOG2E
    return band, jnp.broadcast_to(far[:, None, None], (DSA_HEADS, 1, LANES))


def _dsa_attention(big, mask, rel_bias_table, nbatch, lp):
    t = SEQ_TILE
    nqb = lp // t
    n = big.shape[0]
    pairs = [(b, kt) for b in range(nqb) for kt in range(min(b + 2, nqb))]
    qb_of = jnp.asarray(np.array([p[0] for p in pairs], np.int32))
    kt_of = jnp.asarray(np.array([p[1] for p in pairs], np.int32))
    band, far = _dsa_bias_band(rel_bias_table)
    qcol, kcol, vcol = 4, 5, 6
    qrow = lambda bb, s, qr, kr: (bb * nqb + qr[s], 0)
    grid_spec = pltpu.PrefetchScalarGridSpec(
        num_scalar_prefetch=2,
        grid=(nbatch, len(pairs)),
        in_specs=[pl.BlockSpec((t, D_DSA), lambda bb, s, qr, kr: (bb * nqb + qr[s], qcol)),
                  pl.BlockSpec((t, D_DSA), lambda bb, s, qr, kr: (bb * nqb + kr[s], kcol)),
                  pl.BlockSpec((t, D_DSA), lambda bb, s, qr, kr: (bb * nqb + kr[s], vcol)),
                  pl.BlockSpec((t, t), lambda bb, s, qr, kr: (bb * nqb + qr[s], kr[s])),
                  pl.BlockSpec((3, DSA_HEADS, t, t), lambda bb, s, qr, kr: (0, 0, 0, 0)),
                  pl.BlockSpec((DSA_HEADS, 1, LANES), lambda bb, s, qr, kr: (0, 0, 0))],
        out_specs=pl.BlockSpec((t, D_DSA), qrow),
        scratch_shapes=[pltpu.VMEM((DSA_HEADS, t, t), F32),
                        pltpu.VMEM((DSA_HEADS, t, LANES), F32), pltpu.VMEM((DSA_HEADS, t, LANES), F32),
                        pltpu.VMEM((t, D_DSA), F32)])
    return pl.pallas_call(
        functools.partial(_dsa_attn_kernel, lp=lp),
        grid_spec=grid_spec,
        out_shape=jax.ShapeDtypeStruct((n, D_DSA), BF16),
        compiler_params=_params(("parallel", "arbitrary")),
        name="dsa_attention",
    )(qb_of, kt_of, big, big, big, mask, band, far)


def _router_kernel(h_ref, w_ref, b_ref, info_ref, cnt_ref, carry_ref):
    i = pl.program_id(0)
    tm = h_ref.shape[0]

    @pl.when(i == 0)
    def _():
        carry_ref[...] = jnp.zeros_like(carry_ref)

    xh, xm, xl = _split3(h_ref[...])
    w = w_ref[...]
    wh, wm, wl = _split3(w)
    logits = (_dot(xh, wh) + (_dot(xh, wm) + _dot(xm, wh)) + (_dot(xm, wm) + _dot(xl, wh) + _dot(xh, wl))
              + b_ref[...])
    lane = lax.broadcasted_iota(I32, (tm, LANES), 1)
    logits = jnp.where(lane < N_EXPERTS, logits, -jnp.inf)
    v1 = jnp.max(logits, axis=1, keepdims=True)
    e1 = jnp.min(jnp.where(logits == v1, lane, LANES), axis=1, keepdims=True)
    rest = jnp.where(lane == e1, -jnp.inf, logits)
    v2 = jnp.max(rest, axis=1, keepdims=True)
    e2 = jnp.min(jnp.where(rest == v2, lane, LANES), axis=1, keepdims=True)
    ex = jnp.exp(v2 - v1)
    g1 = 1.0 / (1.0 + ex)
    g2 = ex / (1.0 + ex)
    onehot = jnp.where((lane == e1) | (lane == e2), 1.0, 0.0)
    row = lax.broadcasted_iota(I32, (tm, tm), 0)
    col = lax.broadcasted_iota(I32, (tm, tm), 1)
    before = jnp.where(col < row, 1.0, 0.0).astype(BF16)
    rank = _dot(before, onehot.astype(BF16)) + carry_ref[...]
    r1 = jnp.sum(jnp.where(lane == e1, rank, 0.0), axis=1, keepdims=True)
    r2 = jnp.sum(jnp.where(lane == e2, rank, 0.0), axis=1, keepdims=True)
    carry_ref[...] += jnp.sum(onehot, axis=0, keepdims=True)
    cnt_ref[...] = carry_ref[...]
    info = jnp.where(lane == 0, e1.astype(F32), 0.0)
    info = jnp.where(lane == 1, e2.astype(F32), info)
    info = jnp.where(lane == 2, r1, info)
    info = jnp.where(lane == 3, r2, info)
    info = jnp.where(lane == 4, g1, info)
    info = jnp.where(lane == 5, g2, info)
    info_ref[...] = info


def _router(h, router_w, router_b, tm):
    n, d = h.shape
    wpad = jnp.zeros((d, LANES), F32).at[:, :N_EXPERTS].set(router_w.astype(F32))
    bpad = jnp.zeros((1, LANES), F32).at[0, :N_EXPERTS].set(router_b.astype(F32))
    return pl.pallas_call(
        _router_kernel,
        grid=(n // tm,),
        in_specs=[pl.BlockSpec((tm, d), lambda i: (i, 0)),
                  pl.BlockSpec((d, LANES), lambda i: (0, 0)),
                  pl.BlockSpec((1, LANES), lambda i: (0, 0))],
        out_specs=[pl.BlockSpec((tm, LANES), lambda i: (i, 0)), pl.BlockSpec((1, LANES), lambda i: (0, 0))],
        out_shape=[jax.ShapeDtypeStruct((n, LANES), F32), jax.ShapeDtypeStruct((1, LANES), F32)],
        scratch_shapes=[pltpu.VMEM((1, LANES), F32)],
        compiler_params=_params(("arbitrary",)),
        name="router",
    )(h, wpad, bpad)


def _expert_kernel(be_ref, tok_ref, x_ref, w1_ref, w3_ref, w2_ref, y_ref, rows_ref, xb_ref, acc_ref, sem, *, nf, nblk):
    del be_ref
    i = pl.program_id(0)
    f = pl.program_id(1)

    def copy(blk, slot, r):
        src = x_ref.at[pl.ds(tok_ref[blk * MOE_BLK + r], 1)]
        return pltpu.make_async_copy(src, rows_ref.at[slot, pl.ds(r, 1)], sem.at[slot])

    def gather(blk, slot):
        def body(r, carry):
            copy(blk, slot, r).start()
            return carry
        lax.fori_loop(0, MOE_BLK, body, 0, unroll=8)

    def drain(slot):
        def body(r, carry):
            copy(0, slot, 0).wait()
            return carry
        lax.fori_loop(0, MOE_BLK, body, 0, unroll=8)

    @pl.when(f == 0)
    def _():
        slot = i % 2

        @pl.when(i == 0)
        def _():
            gather(0, 0)

        drain(slot)

        @pl.when(i + 1 < nblk)
        def _():
            gather(i + 1, 1 - slot)

        xb_ref[...] = rows_ref[slot].astype(BF16)
        acc_ref[...] = jnp.zeros_like(acc_ref)

    x = xb_ref[...]
    a = _dot(x, w1_ref[...])
    c = _dot(x, w3_ref[...])
    mid = (a * jax.nn.sigmoid(a) * c).astype(BF16)
    acc_ref[...] += _dot(mid, w2_ref[...])

    @pl.when(f == nf - 1)
    def _():
        y_ref[...] = acc_ref[...]


def _experts(x, slot_tok, blk_expert, w1, w3, w2, tf):
    d = x.shape[1]
    cap = slot_tok.shape[0]
    nblk = cap // MOE_BLK
    nf = w1.shape[2] // tf
    grid_spec = pltpu.PrefetchScalarGridSpec(
        num_scalar_prefetch=2,
        grid=(nblk, nf),
        in_specs=[pl.BlockSpec(memory_space=pl.ANY),
                  pl.BlockSpec((None, d, tf), lambda i, f, be, tk: (be[i], 0, f)),
                  pl.BlockSpec((None, d, tf), lambda i, f, be, tk: (be[i], 0, f)),
                  pl.BlockSpec((None, tf, d), lambda i, f, be, tk: (be[i], f, 0))],
        out_specs=pl.BlockSpec((MOE_BLK, d), lambda i, f, be, tk: (i, 0)),
        scratch_shapes=[pltpu.VMEM((2, MOE_BLK, d), F32), pltpu.VMEM((MOE_BLK, d), BF16),
                        pltpu.VMEM((MOE_BLK, d), F32), pltpu.SemaphoreType.DMA((2,))])
    return pl.pallas_call(
        functools.partial(_expert_kernel, nf=nf, nblk=nblk),
        grid_spec=grid_spec,
        out_shape=jax.ShapeDtypeStruct((cap, d), F32),
        compiler_params=_params(("arbitrary", "arbitrary")),
        name="moe_experts",
    )(blk_expert, slot_tok, x, w1.astype(BF16), w3.astype(BF16), w2.astype(BF16))


def _combine_ln_kernel(d1_ref, d2_ref, yb_ref, h_ref, info_ref, g_ref, b_ref, o_ref, y1_ref, y2_ref, sem):
    tm = h_ref.shape[0]
    base = pl.program_id(0) * tm

    def copy(dest, buf, r):
        return pltpu.make_async_copy(yb_ref.at[pl.ds(dest, 1)], buf.at[pl.ds(r, 1)], sem)

    def issue(r, carry):
        copy(d1_ref[base + r], y1_ref, r).start()
        copy(d2_ref[base + r], y2_ref, r).start()
        return carry

    def drain(r, carry):
        copy(0, y1_ref, 0).wait()
        copy(0, y2_ref, 0).wait()
        return carry

    lax.fori_loop(0, tm, issue, 0, unroll=8)
    lax.fori_loop(0, tm, drain, 0, unroll=8)
    info = info_ref[...]
    moe = info[:, 4:5] * y1_ref[...] + info[:, 5:6] * y2_ref[...]
    o_ref[...] = _layer_norm(ALPHA * h_ref[...] + moe, g_ref[...], b_ref[...])


def _combine_ln(yb, d1, d2, h, info, g, b, tm):
    n, d = h.shape
    grid_spec = pltpu.PrefetchScalarGridSpec(
        num_scalar_prefetch=2,
        grid=(n // tm,),
        in_specs=[pl.BlockSpec(memory_space=pl.ANY),
                  pl.BlockSpec((tm, d), lambda i, a, c: (i, 0)),
                  pl.BlockSpec((tm, LANES), lambda i, a, c: (i, 0)),
                  pl.BlockSpec((1, d), lambda i, a, c: (0, 0)),
                  pl.BlockSpec((1, d), lambda i, a, c: (0, 0))],
        out_specs=pl.BlockSpec((tm, d), lambda i, a, c: (i, 0)),
        scratch_shapes=[pltpu.VMEM((tm, d), F32), pltpu.VMEM((tm, d), F32), pltpu.SemaphoreType.DMA(())])
    return pl.pallas_call(
        _combine_ln_kernel,
        grid_spec=grid_spec,
        out_shape=jax.ShapeDtypeStruct((n, d), F32),
        compiler_params=_params(("arbitrary",)),
        name="moe_combine_ln",
    )(d1, d2, yb, h, info, g.reshape(1, d), b.reshape(1, d))


def _moe_ln(h, router_w, router_b, w1, w3, w2, g, b, tm):
    n, d = h.shape
    info, counts = _router(h, router_w, router_b, tm)
    counts = counts[0, :N_EXPERTS].astype(I32)
    padded = (counts + MOE_BLK - 1) // MOE_BLK * MOE_BLK
    pad_end = jnp.cumsum(padded)
    pad_start = pad_end - padded
    e1, e2 = info[:, 0].astype(I32), info[:, 1].astype(I32)
    d1 = pad_start[e1] + info[:, 2].astype(I32)
    d2 = pad_start[e2] + info[:, 3].astype(I32)
    nblk = -(-(n * TOP_K) // MOE_BLK) + N_EXPERTS
    blk_expert = jnp.minimum(jnp.searchsorted(pad_end, jnp.arange(nblk, dtype=I32) * MOE_BLK, side='right'),
                             N_EXPERTS - 1).astype(I32)
    tok = jnp.arange(n, dtype=I32)
    slot_tok = jnp.zeros((nblk * MOE_BLK,), I32).at[jnp.concatenate([d1, d2])].set(
        jnp.concatenate([tok, tok]), unique_indices=True)
    yb = _experts(h, slot_tok, blk_expert, w1, w3, w2, 512)
    return _combine_ln(yb, d1, d2, h, info, g, b, tm)


def kernel(x, meta_tokens, w_in_even, conv_w, s5_a_re, s5_a_im, s5_log_dt, s5_b_re, s5_b_im, s5_c_re, s5_c_im, s5_d, s5_w_glu, w_out_even, ln_mix_even_g, ln_mix_even_b, ffn_w1, ffn_w3, ffn_w2, ln_ffn_even_g, ln_ffn_even_b, w_in_odd, mlstm_b_i, mlstm_b_f, rel_bias_table, w_out_odd, ln_mix_odd_g, ln_mix_odd_b, router_w, router_b, expert_w1, expert_w3, expert_w2, ln_ffn_odd_g, ln_ffn_odd_b):
    bsz, seq, d = x.shape
    topk = min(TOPK_MAX, seq // 4)
    lh = seq + N_META
    lp = -(-lh // SEQ_TILE) * SEQ_TILE
    n = bsz * lp
    tm = _pick(n, (1280, 640, 512, 256))
    tln = _pick(n, (512, 256))
    tseq = _pick(lp, (1280, 640, 256))

    h = jnp.concatenate([jnp.broadcast_to(meta_tokens.astype(x.dtype)[None], (bsz, N_META, d)), x,
                         jnp.zeros((bsz, lp - lh, d), x.dtype)], axis=1).reshape(n, d)
    hb = h.astype(BF16)

    p = _proj(hb, w_in_even[0].astype(BF16), F32, tm, 1280)
    y_conv = _conv_mixer(p, conv_w[0].astype(F32), lp, tseq, 512)
    ops = _s5_operators(s5_a_re[0], s5_a_im[0], s5_log_dt[0], s5_b_re[0], s5_b_im[0], s5_c_re[0], s5_c_im[0], S5_T)
    u = p[:, 3 * D_CONV:].astype(BF16)
    y_lin = _s5_linear(u, ops, bsz, lp)
    y_s5 = _s5_glu(y_lin, p, s5_d[0], s5_w_glu[0], tm)
    h, hb = _outproj_ln(y_conv, y_s5, w_out_even[0], h, ln_mix_even_g[0], ln_mix_even_b[0], tln)
    h, hb = _ffn_ln(h, ffn_w1[0], ffn_w3[0], ffn_w2[0], ln_ffn_even_g[0], ln_ffn_even_b[0], tln, 512)

    w = w_in_odd[0]
    o = np.cumsum([0] + [D_MLSTM] * 4 + [MLSTM_HEADS] * 2 + [D_DSA] * 3 + [IDX_HEADS * IDX_DH, IDX_DH, IDX_HEADS])
    seg = lambda j: w[:, o[j]:o[j + 1]]
    w_big = jnp.concatenate([seg(0), seg(1), seg(2), seg(3), seg(6), seg(7), seg(8), seg(9)], axis=1)
    w_small = jnp.concatenate([seg(4), seg(5), seg(10), seg(11)], axis=1)
    w_small = jnp.pad(w_small, ((0, 0), (0, LANES - w_small.shape[1])))
    big = _proj(hb, w_big.astype(BF16), BF16, tm, 1280)
    small = _proj(hb, w_small.astype(BF16), F32, tm, LANES)
    gate_bias = jnp.zeros((1, LANES), F32).at[0, :MLSTM_HEADS].set(mlstm_b_i[0].astype(F32))
    gate_bias = gate_bias.at[0, MLSTM_HEADS:2 * MLSTM_HEADS].set(mlstm_b_f[0].astype(F32))
    y_m = _mlstm(big, small, gate_bias, bsz, lp)
    qi = big[:, 7 * D_DSA:].reshape(bsz, lp, IDX_HEADS, IDX_DH).transpose(0, 2, 1, 3)
    kidx = small[:, 2 * MLSTM_HEADS:2 * MLSTM_HEADS + IDX_DH].astype(BF16).reshape(bsz, lp, IDX_DH)
    widx = small[:, 2 * MLSTM_HEADS + IDX_DH:2 * MLSTM_HEADS + IDX_DH + IDX_HEADS]
    mask = _dsa_select(qi, kidx, widx, bsz, lp, topk)
    y_d = _dsa_attention(big, mask, rel_bias_table, bsz, lp)
    h, hb = _outproj_ln(y_m, y_d, w_out_odd[0], h, ln_mix_odd_g[0], ln_mix_odd_b[0], tln)
    h = _moe_ln(h, router_w[0], router_b[0], expert_w1[0], expert_w3[0], expert_w2[0],
                ln_ffn_odd_g[0], ln_ffn_odd_b[0], tln)
    return h.reshape(bsz, lp, d)[:, N_META:lh]
```

```python
import functools
import math

import jax
import jax.numpy as jnp
import numpy as np
from jax import lax
from jax.experimental import pallas as pl
from jax.experimental.pallas import tpu as pltpu

F32 = jnp.float32
BF16 = jnp.bfloat16
I32 = jnp.int32
I16 = jnp.int16

D_MODEL = 2048
DEPTH = 2
CHUNK = 64
N_META = 16
CONV_WIDTH = 3
D_CONV = 1536
S5_CH = 16
S5_STATE = 64
D_SSM = 512
S5_GROUPS = D_SSM // S5_CH
MLSTM_HEADS = 4
MLSTM_DH = 256
D_MLSTM = MLSTM_HEADS * MLSTM_DH
DSA_HEADS = 8
DSA_DH = 128
D_DSA = DSA_HEADS * DSA_DH
IDX_HEADS = 8
IDX_DH = 64
TOPK_MAX = 256
NUM_BUCKETS = 32
MAX_DISTANCE = 128
D_FF = 5632
N_EXPERTS = 8
TOP_K = 2
D_EXPERT = 5632
ALPHA = (2 * DEPTH) ** 0.25
LN_EPS = 1e-5

LANES = 128
SEQ_TILE = 256
S5_T = 64
SEARCH_W = 1024
MOE_BLK = 512
VMEM_LIMIT = 56 * 1024 * 1024
NEG = -1e30
LOG2E = math.log2(math.e)
INT_MIN = -(2 ** 31)


def _params(sem, vmem=VMEM_LIMIT):
    return pltpu.CompilerParams(dimension_semantics=sem, vmem_limit_bytes=vmem)


def _pick(n, candidates):
    for c in candidates:
        if n % c == 0:
            return c
    raise ValueError(f"no tile in {candidates} divides {n}")


def _dot(a, b):
    return jnp.dot(a, b, preferred_element_type=F32)


def _dot_nt(a, b):
    return lax.dot_general(a, b, (((1,), (1,)), ((), ())), preferred_element_type=F32)


def _split3(x):
    hi = x.astype(BF16)
    r = x - hi.astype(F32)
    mid = r.astype(BF16)
    lo = (r - mid.astype(F32)).astype(BF16)
    return hi, mid, lo


def _layer_norm(z, g, b):
    mu = jnp.mean(z, axis=-1, keepdims=True)
    zc = z - mu
    var = jnp.mean(zc * zc, axis=-1, keepdims=True)
    return zc * lax.rsqrt(var + LN_EPS) * g + b


def _proj_kernel(x_ref, w_ref, o_ref):
    o_ref[...] = _dot(x_ref[...], w_ref[...]).astype(o_ref.dtype)


def _proj(x, w, out_dtype, tm, tn):
    n, k = x.shape
    m = w.shape[1]
    return pl.pallas_call(
        _proj_kernel,
        grid=(n // tm, m // tn),
        in_specs=[pl.BlockSpec((tm, k), lambda i, j: (i, 0)),
                  pl.BlockSpec((k, tn), lambda i, j: (0, j))],
        out_specs=pl.BlockSpec((tm, tn), lambda i, j: (i, j)),
        out_shape=jax.ShapeDtypeStruct((n, m), out_dtype),
        compiler_params=_params(("parallel", "parallel")),
        name="proj",
    )(x, w)


def _outproj_ln_kernel(ya_ref, yb_ref, wa_ref, wb_ref, h_ref, g_ref, b_ref, o_ref, ob_ref):
    mix = _dot(ya_ref[...], wa_ref[...]) + _dot(yb_ref[...], wb_ref[...])
    out = _layer_norm(ALPHA * h_ref[...] + mix, g_ref[...], b_ref[...])
    o_ref[...] = out
    ob_ref[...] = out.astype(BF16)


def _outproj_ln(ya, yb, w, h, g, b, tm):
    n, d = h.shape
    ka, kb = ya.shape[1], yb.shape[1]
    wa, wb = w[:ka].astype(BF16), w[ka:].astype(BF16)
    row = lambda i: (i, 0)
    fixed = lambda i: (0, 0)
    return pl.pallas_call(
        _outproj_ln_kernel,
        grid=(n // tm,),
        in_specs=[pl.BlockSpec((tm, ka), row), pl.BlockSpec((tm, kb), row),
                  pl.BlockSpec((ka, d), fixed), pl.BlockSpec((kb, d), fixed),
                  pl.BlockSpec((tm, d), row),
                  pl.BlockSpec((1, d), fixed), pl.BlockSpec((1, d), fixed)],
        out_specs=[pl.BlockSpec((tm, d), row), pl.BlockSpec((tm, d), row)],
        out_shape=[jax.ShapeDtypeStruct((n, d), F32), jax.ShapeDtypeStruct((n, d), BF16)],
        compiler_params=_params(("parallel",)),
        name="outproj_ln",
    )(ya, yb, wa, wb, h, g.reshape(1, d), b.reshape(1, d))


def _conv_kernel(b_ref, c_ref, x_ref, ch_ref, xh_ref, w_ref, y_ref, *, tiles_per_seq):
    i = pl.program_id(0)
    z = c_ref[...] * x_ref[...]
    zh = ch_ref[...] * xh_ref[...]
    zh = jnp.where(i % tiles_per_seq == 0, 0.0, zh)
    r = lax.broadcasted_iota(I32, z.shape, 0)
    z1 = jnp.where(r == 0, zh[7:8], pltpu.roll(z, 1, 0))
    z2 = jnp.where(r == 0, zh[6:7], jnp.where(r == 1, zh[7:8], pltpu.roll(z, 2, 0)))
    w = w_ref[...]
    conv = w[0:1] * z2 + w[1:2] * z1 + w[2:3] * z
    y_ref[...] = (b_ref[...] * conv).astype(y_ref.dtype)


def _conv_mixer(p, conv_w, lp, tm, tc):
    n = p.shape[0]
    ncb = D_CONV // tc
    hb = tm // 8
    body = functools.partial(_conv_kernel, tiles_per_seq=lp // tm)
    return pl.pallas_call(
        body,
        grid=(n // tm, ncb),
        in_specs=[pl.BlockSpec((tm, tc), lambda i, j: (i, j)),
                  pl.BlockSpec((tm, tc), lambda i, j: (i, ncb + j)),
                  pl.BlockSpec((tm, tc), lambda i, j: (i, 2 * ncb + j)),
                  pl.BlockSpec((8, tc), lambda i, j: (jnp.maximum(i * hb - 1, 0), ncb + j)),
                  pl.BlockSpec((8, tc), lambda i, j: (jnp.maximum(i * hb - 1, 0), 2 * ncb + j)),
                  pl.BlockSpec((CONV_WIDTH, tc), lambda i, j: (0, j))],
        out_specs=pl.BlockSpec((tm, tc), lambda i, j: (i, j)),
        out_shape=jax.ShapeDtypeStruct((n, D_CONV), BF16),
        compiler_params=_params(("parallel", "parallel")),
        name="conv_mixer",
    )(p, p, p, p, p, conv_w)


def _s5_operators(a_re, a_im, log_dt, b_re, b_im, c_re, c_im, t):
    a = lax.complex(a_re.astype(F32), a_im.astype(F32))
    dt = jnp.exp(log_dt.astype(F32))[:, None]
    a_bar = jnp.exp(dt * a)
    b_bar = ((a_bar - 1.0) / a)[:, :, None] * lax.complex(b_re.astype(F32), b_im.astype(F32))
    c = lax.complex(c_re.astype(F32), c_im.astype(F32))
    lag = jnp.arange(t + 1, dtype=F32)[:, None, None]
    pw = jnp.exp(lag * (dt * a)[None])
    kern = jnp.real(jnp.einsum('gcp,lgp,gpd->glcd', c, pw[:t], b_bar))
    lagged = jnp.concatenate([jnp.zeros_like(kern), kern], axis=1)
    blocks = jnp.stack([lagged[:, t - j:2 * t - j] for j in range(t)], axis=1)
    g = a.shape[0]
    toep = blocks.transpose(0, 1, 4, 2, 3).reshape(g, t * S5_CH, t * S5_CH)
    wc = pw[:t][::-1].transpose(1, 0, 2)[:, :, None, :] * b_bar.transpose(0, 2, 1)[:, None, :, :]
    wc = wc.reshape(g, t * S5_CH, S5_STATE)
    w_in = jnp.concatenate([jnp.real(wc), jnp.imag(wc)], axis=-1)
    cp = c.transpose(0, 2, 1)[:, :, None, :] * pw[1:t + 1].transpose(1, 2, 0)[:, :, :, None]
    cp = cp.reshape(g, S5_STATE, t * S5_CH)
    v_out = jnp.concatenate([jnp.real(cp), -jnp.imag(cp)], axis=1)
    a_t = jnp.stack([jnp.real(pw[t]), jnp.imag(pw[t])], axis=1)
    return toep.astype(BF16), w_in.astype(BF16), v_out.astype(BF16), a_t


def _s5_kernel(u_ref, toep_ref, win_ref, vout_ref, at_ref, y_ref, er_ref, ei_ref, hr_ref, hi_ref, *, nbatch, nc):
    u = u_ref[...]
    e = _dot(u, win_ref[...])
    er_ref[...] = e[:, :S5_STATE]
    ei_ref[...] = e[:, S5_STATE:]
    at = at_ref[...]
    ar, ai = at[0:1], at[1:2]

    def step(c, carry):
        out = []
        for b in range(nbatch):
            hr, hi = carry[2 * b], carry[2 * b + 1]
            row = b * nc + c
            hr_ref[pl.ds(row, 1), :] = hr
            hi_ref[pl.ds(row, 1), :] = hi
            er = er_ref[pl.ds(row, 1), :]
            ei = ei_ref[pl.ds(row, 1), :]
            out += [ar * hr - ai * hi + er, ar * hi + ai * hr + ei]
        return tuple(out)

    zero = jnp.zeros((1, S5_STATE), F32)
    lax.fori_loop(0, nc, step, (zero,) * (2 * nbatch))
    vout = vout_ref[...]
    y = _dot(u, toep_ref[...])
    for part, rows in ((hr_ref[...], vout[:S5_STATE]), (hi_ref[...], vout[S5_STATE:])):
        hi_, mid_, _ = _split3(part)
        y += _dot(hi_, rows) + _dot(mid_, rows)
    y_ref[...] = y


def _s5_linear(u, ops, nbatch, lp):
    toep, w_in, v_out, a_t = ops
    n = u.shape[0]
    t = S5_T
    nc = lp // t
    rows = nbatch * nc
    width = t * S5_CH
    ug = u.reshape(rows, t, S5_GROUPS, S5_CH).transpose(2, 0, 1, 3).reshape(S5_GROUPS, rows, width)
    grp = lambda g: (g, 0, 0)
    yg = pl.pallas_call(
        functools.partial(_s5_kernel, nbatch=nbatch, nc=nc),
        grid=(S5_GROUPS,),
        in_specs=[pl.BlockSpec((None, rows, width), grp),
                  pl.BlockSpec((None, width, width), grp),
                  pl.BlockSpec((None, width, 2 * S5_STATE), grp),
                  pl.BlockSpec((None, 2 * S5_STATE, width), grp),
                  pl.BlockSpec((None, 2, S5_STATE), grp)],
        out_specs=pl.BlockSpec((None, rows, width), grp),
        out_shape=jax.ShapeDtypeStruct((S5_GROUPS, rows, width), F32),
        scratch_shapes=[pltpu.VMEM((rows, S5_STATE), F32)] * 4,
        compiler_params=_params(("parallel",)),
        name="s5_linear",
    )(ug, toep, w_in, v_out, a_t)
    return yg.reshape(S5_GROUPS, rows, t, S5_CH).transpose(1, 2, 0, 3).reshape(n, D_SSM)


def _s5_glu_kernel(y_ref, u_ref, d_ref, w_ref, o_ref):
    y = y_ref[...] + d_ref[...] * u_ref[...]
    y = jax.nn.gelu(y)
    hi, mid, _ = _split3(y)
    w = w_ref[...]
    gate = _dot(hi, w) + _dot(mid, w)
    o_ref[...] = (y * jax.nn.sigmoid(gate)).astype(o_ref.dtype)


def _s5_glu(y_lin, p, d_skip, w_glu, tm):
    n = y_lin.shape[0]
    ucol = (3 * D_CONV) // D_SSM
    wbd = jnp.einsum('gcd,gh->gdhc', w_glu.astype(F32), jnp.eye(S5_GROUPS, dtype=F32)).reshape(D_SSM, D_SSM)
    return pl.pallas_call(
        _s5_glu_kernel,
        grid=(n // tm,),
        in_specs=[pl.BlockSpec((tm, D_SSM), lambda i: (i, 0)),
                  pl.BlockSpec((tm, D_SSM), lambda i: (i, ucol)),
                  pl.BlockSpec((1, D_SSM), lambda i: (0, 0)),
                  pl.BlockSpec((D_SSM, D_SSM), lambda i: (0, 0))],
        out_specs=pl.BlockSpec((tm, D_SSM), lambda i: (i, 0)),
        out_shape=jax.ShapeDtypeStruct((n, D_SSM), BF16),
        compiler_params=_params(("parallel",)),
        name="s5_glu",
    )(y_lin, p, d_skip.reshape(1, D_SSM).astype(F32), wbd.astype(BF16))


def _ffn_ln_kernel(h_ref, w1_ref, w3_ref, w2_ref, g_ref, b_ref, o_ref, ob_ref, xb_ref, acc_ref, *, nf):
    f = pl.program_id(1)

    @pl.when(f == 0)
    def _():
        xb_ref[...] = h_ref[...].astype(BF16)
        acc_ref[...] = jnp.zeros_like(acc_ref)

    x = xb_ref[...]
    a = _dot(x, w1_ref[...])
    c = _dot(x, w3_ref[...])
    mid = (a * jax.nn.sigmoid(a) * c).astype(BF16)
    acc_ref[...] += _dot(mid, w2_ref[...])

    @pl.when(f == nf - 1)
    def _():
        out = _layer_norm(ALPHA * h_ref[...] + acc_ref[...], g_ref[...], b_ref[...])
        o_ref[...] = out
        ob_ref[...] = out.astype(BF16)


def _ffn_ln(h, w1, w3, w2, g, b, tm, tf):
    n, d = h.shape
    nf = w1.shape[1] // tf
    return pl.pallas_call(
        functools.partial(_ffn_ln_kernel, nf=nf),
        grid=(n // tm, nf),
        in_specs=[pl.BlockSpec((tm, d), lambda i, f: (i, 0)),
                  pl.BlockSpec((d, tf), lambda i, f: (0, f)),
                  pl.BlockSpec((d, tf), lambda i, f: (0, f)),
                  pl.BlockSpec((tf, d), lambda i, f: (f, 0)),
                  pl.BlockSpec((1, d), lambda i, f: (0, 0)),
                  pl.BlockSpec((1, d), lambda i, f: (0, 0))],
        out_specs=[pl.BlockSpec((tm, d), lambda i, f: (i, 0)), pl.BlockSpec((tm, d), lambda i, f: (i, 0))],
        out_shape=[jax.ShapeDtypeStruct((n, d), F32), jax.ShapeDtypeStruct((n, d), BF16)],
        scratch_shapes=[pltpu.VMEM((tm, d), BF16), pltpu.VMEM((tm, d), F32)],
        compiler_params=_params(("parallel", "arbitrary")),
        name="ffn_ln",
    )(h, w1.astype(BF16), w3.astype(BF16), w2.astype(BF16), g.reshape(1, d), b.reshape(1, d))


def _log_sigmoid(x):
    return jnp.minimum(x, 0.0) - jnp.log(1.0 + jnp.exp(-jnp.abs(x)))


def _mlstm_kernel(q_ref, k_ref, v_ref, o_ref, g_ref, gb_ref, y_ref, ct_ref, n_ref, m_ref):
    c = pl.program_id(1)
    t = q_ref.shape[0]
    dh = MLSTM_DH

    @pl.when(c == 0)
    def _():
        ct_ref[...] = jnp.zeros_like(ct_ref)
        n_ref[...] = jnp.zeros_like(n_ref)
        m_ref[...] = jnp.zeros_like(m_ref)

    gates = g_ref[...] + gb_ref[...]
    lf = _log_sigmoid(gates)
    row = lax.broadcasted_iota(I32, (t, t), 0)
    col = lax.broadcasted_iota(I32, (t, t), 1)
    tril = col <= row
    tril_b = jnp.where(tril, 1.0, 0.0).astype(BF16)
    hi, mid, lo = _split3(lf)
    bcum = _dot(tril_b, hi) + _dot(tril_b, mid) + _dot(tril_b, lo)
    gates_t = gates.T
    bcum_t = bcum.T
    for h in range(MLSTM_HEADS):
        sl = slice(h * dh, (h + 1) * dh)
        fh = MLSTM_HEADS + h
        li_row, li_col = gates_t[h:h + 1, :], gates[:, h:h + 1]
        bc_row, bc_col = bcum_t[fh:fh + 1, :], bcum[:, fh:fh + 1]
        m_prev = m_ref[h:h + 1, 0:1]
        dmat = jnp.where(tril, bc_col - bc_row + li_row, -jnp.inf)
        m_inter = bc_col + m_prev
        m_t = jnp.maximum(m_inter, jnp.max(dmat, axis=1, keepdims=True))
        q = q_ref[:, sl]
        k = k_ref[:, sl] * (dh ** -0.5)
        v = v_ref[:, sl]
        s = _dot_nt(q, k) * jnp.exp(dmat - m_t)
        w_inter = jnp.exp(m_inter - m_t)
        ct = ct_ref[h]
        nvec = n_ref[h:h + 1, :]
        num = _dot(s.astype(BF16), v) + w_inter * _dot(q, ct.astype(BF16))
        den = jnp.sum(s, axis=1, keepdims=True) + w_inter * jnp.sum(q.astype(F32) * nvec, axis=1, keepdims=True)
        h_out = num / jnp.maximum(jnp.abs(den), jnp.exp(-m_t))
        y_ref[:, sl] = (jax.nn.sigmoid(o_ref[:, sl].astype(F32)) * h_out).astype(y_ref.dtype)
        b_last = bc_col[t - 1:t, :]
        m_new = jnp.maximum(b_last + m_prev, jnp.max(b_last - bc_row + li_row, axis=1, keepdims=True))
        decay = jnp.exp(b_last + m_prev - m_new)
        kw = k.astype(F32) * jnp.exp(b_last - bc_col + li_col - m_new)
        ct_ref[h] = decay * ct + _dot(kw.T.astype(BF16), v)
        n_ref[h:h + 1, :] = decay * nvec + jnp.sum(kw, axis=0, keepdims=True)
        m_ref[h:h + 1, :] = jnp.broadcast_to(m_new, (1, LANES))


def _mlstm(big, small, gate_bias, nbatch, lp):
    n = big.shape[0]
    t = SEQ_TILE
    nc = lp // t
    blk = lambda j: pl.BlockSpec((t, D_MLSTM), lambda b, c: (b * nc + c, j))
    return pl.pallas_call(
        _mlstm_kernel,
        grid=(nbatch, nc),
        in_specs=[blk(0), blk(1), blk(2), blk(3),
                  pl.BlockSpec((t, LANES), lambda b, c: (b * nc + c, 0)),
                  pl.BlockSpec((1, LANES), lambda b, c: (0, 0))],
        out_specs=pl.BlockSpec((t, D_MLSTM), lambda b, c: (b * nc + c, 0)),
        out_shape=jax.ShapeDtypeStruct((n, D_MLSTM), BF16),
        scratch_shapes=[pltpu.VMEM((MLSTM_HEADS, MLSTM_DH, MLSTM_DH), F32),
                        pltpu.VMEM((8, MLSTM_DH), F32),
                        pltpu.VMEM((8, LANES), F32)],
        compiler_params=_params(("parallel", "arbitrary")),
        name="mlstm",
    )(big, big, big, big, small, gate_bias)


def _chunk_end(pos):
    return jnp.where(pos < N_META, N_META, N_META + CHUNK + CHUNK * ((pos - N_META) >> 6))


def _t5_bucket(rel):
    half = NUM_BUCKETS // 2
    max_exact = half // 2
    ret = jnp.where(rel > 0, half, 0)
    n = jnp.abs(rel)
    nf = jnp.maximum(n, 1).astype(F32)
    large = max_exact + (jnp.log(nf / max_exact) / math.log(MAX_DISTANCE / max_exact) * (half - max_exact)).astype(I32)
    large = jnp.minimum(large, half - 1)
    return ret + jnp.where(n < max_exact, n, large)


def _f32_order_key(x):
    bits = lax.bitcast_convert_type(x, I32)
    return bits ^ ((bits >> 31) & 0x7FFFFFFF)


def _lanes(x, width):
    return jnp.concatenate([x] * (width // LANES), axis=1)


def _dsa_select_kernel(qi_ref, kidx_ref, w_ref, mask_ref, hi_ref, lo_ref, wb_ref, lim_ref, *, lp, topk):
    qb, sw = SEQ_TILE, SEARCH_W
    b = pl.program_id(1)
    nkt = jnp.minimum(b + 2, lp // qb)
    nst = (nkt + sw // qb - 1) // (sw // qb)
    wi = w_ref[...] * (IDX_HEADS ** -0.5) * (IDX_DH ** -0.5)
    for h in range(IDX_HEADS):
        wb_ref[h] = jnp.broadcast_to(wi[:, h:h + 1], (qb, LANES))
    end = _chunk_end(b * qb + lax.broadcasted_iota(I32, (qb, 1), 0))
    lane = lax.broadcasted_iota(I32, (qb, qb), 1)
    low16 = jnp.int16(-(2 ** 15))

    def store_keys(off, key):
        hi_ref[:, pl.ds(off, qb)] = (key >> 16).astype(I16)
        lo_ref[:, pl.ds(off, qb)] = key.astype(I16) ^ low16

    qi_all = qi_ref[...].reshape(IDX_HEADS * qb, IDX_DH)

    def fill(kt, visible_only):
        off = pl.multiple_of(kt * qb, qb)
        s_all = _dot(qi_all, kidx_ref[:, pl.ds(off, qb)])
        score = None
        for h in range(IDX_HEADS):
            term = _lanes(wb_ref[h], qb) * jnp.maximum(s_all[h * qb:(h + 1) * qb], 0.0)
            score = term if score is None else score + term
        key = _f32_order_key(score)
        if visible_only:
            key = jnp.where(off + lane < end, key, INT_MIN)
        store_keys(off, key)

    def loop(lo, hi, body):
        lax.fori_loop(lo, hi, lambda i, c: (body(i), c)[1], 0)

    loop(0, jnp.minimum(b, nkt), lambda kt: fill(kt, False))
    loop(jnp.minimum(b, nkt), nkt, lambda kt: fill(kt, True))
    loop(nkt, nst * (sw // qb), lambda kt: store_keys(pl.multiple_of(kt * qb, qb), jnp.full((qb, qb), INT_MIN, I32)))

    one = jnp.ones((qb, sw), BF16)
    zero = jnp.zeros((qb, sw), BF16)
    ones_rhs = jnp.ones((LANES, LANES), BF16)

    def count(*preds):
        def body(st, accs):
            off = pl.multiple_of(st * sw, sw)
            hi, lo = hi_ref[:, pl.ds(off, sw)], lo_ref[:, pl.ds(off, sw)]
            out = []
            for pred, acc in zip(preds, accs):
                hit = pred(hi, lo, off)
                parts = [hit[:, j * LANES:(j + 1) * LANES] for j in range(sw // LANES)]
                while len(parts) > 1:
                    parts = [parts[j] + parts[j + 1] for j in range(0, len(parts), 2)]
                out.append(acc + parts[0])
            return tuple(out)
        accs = lax.fori_loop(0, nst, body, (jnp.zeros((qb, LANES), BF16),) * len(preds))
        return [_dot(acc, ones_rhs) for acc in accs]

    def as16(u):
        return _lanes((u - 2 ** 15).astype(I16), sw)

    def search(pred_ge, target):
        def rnd(i, u):
            cand = u | jnp.left_shift(jnp.int32(1), 15 - i)
            cnt, = count(pred_ge(as16(cand)))
            return jnp.where(cnt >= target, cand, u)
        return lax.fori_loop(0, 16, rnd, jnp.zeros((qb, LANES), I32))

    kf = jnp.float32(topk)
    t_hi = search(lambda c: lambda hi, lo, off: jnp.where(hi >= c, one, zero), kf)
    thi16 = as16(t_hi)
    above_hi, = count(lambda hi, lo, off: jnp.where(hi > thi16, one, zero))

    def restrict(st):
        off = pl.multiple_of(st * sw, sw)
        lo_ref[:, pl.ds(off, sw)] = jnp.where(hi_ref[:, pl.ds(off, sw)] == thi16, lo_ref[:, pl.ds(off, sw)], low16)

    loop(0, nst, restrict)
    t_lo = search(lambda c: lambda hi, lo, off: jnp.where(lo >= c, one, zero), kf - above_hi)
    tlo16 = as16(t_lo)
    above_lo, equal = count(lambda hi, lo, off: jnp.where(lo > tlo16, one, zero),
                            lambda hi, lo, off: jnp.where(hi == thi16, jnp.where(lo == tlo16, one, zero), zero))
    above = above_hi + above_lo
    lowest = (t_hi == 0) & (t_lo == 0)
    tie = (above + equal > kf) & jnp.logical_not(lowest)
    lim_ref[...] = jnp.full((qb, LANES), lp, I32)

    def col16(off, width):
        return (off + lax.broadcasted_iota(I32, (qb, width), 1)).astype(I16)

    @pl.when(jnp.max(jnp.where(tie, 1, 0)) > 0)
    def _():
        need = kf - above

        def rnd(i, x):
            cand = x | jnp.left_shift(jnp.int32(1), 14 - i)
            c16 = _lanes(cand.astype(I16), sw)
            cnt, = count(lambda hi, lo, off: jnp.where(
                hi == thi16, jnp.where(lo == tlo16, jnp.where(col16(off, sw) < c16, one, zero), zero), zero))
            return jnp.where(cnt < need, cand, x)

        x = lax.fori_loop(0, 15, rnd, jnp.zeros((qb, LANES), I32))
        lim_ref[...] = jnp.where(tie, x, lp)

    thi_t, tlo_t, lim_t = (_lanes(v.astype(I16), qb) for v in (t_hi - 2 ** 15, t_lo - 2 ** 15, lim_ref[...]))
    keep = jnp.zeros((qb, qb), BF16)
    drop = jnp.full((qb, qb), NEG, BF16)

    def emit(kt):
        off = pl.multiple_of(kt * qb, qb)
        hi = hi_ref[:, pl.ds(off, qb)]
        lo = lo_ref[:, pl.ds(off, qb)]
        at_thr = jnp.where(lo == tlo_t, jnp.where(col16(off, qb) <= lim_t, keep, drop), drop)
        in_lo = jnp.where(lo > tlo_t, keep, at_thr)
        mask_ref[:, pl.ds(off, qb)] = jnp.where(hi > thi_t, keep, jnp.where(hi == thi_t, in_lo, drop))

    def blank(kt):
        mask_ref[:, pl.ds(pl.multiple_of(kt * qb, qb), qb)] = drop

    loop(0, nkt, emit)
    loop(nkt, lp // qb, blank)


def _dsa_select(qi, kidx, w, nbatch, lp, topk):
    qb = SEQ_TILE
    nqb = lp // qb
    lpad = -(-lp // SEARCH_W) * SEARCH_W
    assert lpad <= 256 * LANES
    return pl.pallas_call(
        functools.partial(_dsa_select_kernel, lp=lp, topk=topk),
        grid=(nbatch, nqb),
        in_specs=[pl.BlockSpec((None, IDX_HEADS, qb, IDX_DH), lambda bb, b: (bb, 0, b, 0)),
                  pl.BlockSpec((None, IDX_DH, lp), lambda bb, b: (bb, 0, 0)),
                  pl.BlockSpec((qb, IDX_HEADS), lambda bb, b: (bb * nqb + b, 0))],
        out_specs=pl.BlockSpec((qb, lp), lambda bb, b: (bb * nqb + b, 0)),
        out_shape=jax.ShapeDtypeStruct((nbatch * lp, lp), BF16),
        scratch_shapes=[pltpu.VMEM((qb, lpad), I16), pltpu.VMEM((qb, lpad), I16),
                        pltpu.VMEM((IDX_HEADS, qb, LANES), F32), pltpu.VMEM((qb, LANES), I32)],
        compiler_params=_params(("parallel", "parallel")),
        name="dsa_select",
    )(qi, kidx, w)


def _dsa_attn_kernel(qb_ref, kt_ref, q_ref, k_ref, v_ref, mask_ref, band_ref, far_ref, y_ref,
                     s_ref, m_ref, l_ref, acc_ref, *, lp):
    t = SEQ_TILE
    step = pl.program_id(1)
    b = qb_ref[step]
    kt = kt_ref[step]
    last_kt = jnp.minimum(b + 1, lp // t - 1)

    @pl.when(kt == 0)
    def _():
        m_ref[...] = jnp.full_like(m_ref, NEG)
        l_ref[...] = jnp.zeros_like(l_ref)
        acc_ref[...] = jnp.zeros_like(acc_ref)

    heads = [slice(h * DSA_DH, (h + 1) * DSA_DH) for h in range(DSA_HEADS)]
    mask = mask_ref[...].astype(F32)
    scale = DSA_DH ** -0.5 * LOG2E
    for h, sl in enumerate(heads):
        s_ref[h] = _dot_nt(q_ref[:, sl], k_ref[:, sl]) * scale + mask

    ones_v = jnp.ones((t, LANES), BF16)

    def update(z, shift):
        m_prev = m_ref[...]
        m_next = jnp.maximum(m_prev, jnp.max(z, axis=2, keepdims=True) + shift)
        p = jnp.exp2(z - jnp.concatenate([m_next - shift] * (t // LANES), axis=2)).astype(BF16)
        alpha = jnp.exp2(m_prev - m_next)
        m_ref[...] = m_next
        for h, sl in enumerate(heads):
            pv = _dot(p[h], jnp.concatenate([v_ref[:, sl], ones_v], axis=1))
            acc_ref[:, sl] = alpha[h] * acc_ref[:, sl] + pv[:, :DSA_DH]
            l_ref[h] = alpha[h] * l_ref[h] + pv[:, DSA_DH:]

    near = kt >= b - 1

    @pl.when(near)
    def _():
        update(s_ref[...] + band_ref[kt - b + 1], jnp.zeros((DSA_HEADS, 1, LANES), F32))

    @pl.when(jnp.logical_not(near))
    def _():
        update(s_ref[...], far_ref[...])

    @pl.when(kt == last_kt)
    def _():
        for h, sl in enumerate(heads):
            y_ref[:, sl] = (acc_ref[:, sl] / l_ref[h]).astype(y_ref.dtype)


def _dsa_bias_band(rel_bias_table):
    t = SEQ_TILE
    rel = jnp.arange(-2 * t + 1, 2 * t, dtype=I32)
    by_rel = (rel_bias_table.astype(F32)[_t5_bucket(rel)] * LOG2E).T
    bias = jnp.stack([by_rel[:, t - 1 - i:4 * t - 1 - i] for i in range(t)], axis=1)
    i = jnp.arange(t, dtype=I32)[:, None]
    c = jnp.arange(3 * t, dtype=I32)[None, :] - t
    band = jnp.where((c < _chunk_end(i))[None], bias, NEG)
    band = band.reshape(DSA_HEADS, t, 3, t).transpose(2, 0, 1, 3)
    far = rel_bias_table.astype(F32)[_t5_bucket(jnp.int32(-(t + 1)))] * LOG2E
    return band, jnp.broadcast_to(far[:, None, None], (DSA_HEADS, 1, LANES))


def _dsa_attention(big, mask, rel_bias_table, nbatch, lp):
    t = SEQ_TILE
    nqb = lp // t
    n = big.shape[0]
    pairs = [(b, kt) for b in range(nqb) for kt in range(min(b + 2, nqb))]
    qb_of = jnp.asarray(np.array([p[0] for p in pairs], np.int32))
    kt_of = jnp.asarray(np.array([p[1] for p in pairs], np.int32))
    band, far = _dsa_bias_band(rel_bias_table)
    qcol, kcol, vcol = 4, 5, 6
    qrow = lambda bb, s, qr, kr: (bb * nqb + qr[s], 0)
    grid_spec = pltpu.PrefetchScalarGridSpec(
        num_scalar_prefetch=2,
        grid=(nbatch, len(pairs)),
        in_specs=[pl.BlockSpec((t, D_DSA), lambda bb, s, qr, kr: (bb * nqb + qr[s], qcol)),
                  pl.BlockSpec((t, D_DSA), lambda bb, s, qr, kr: (bb * nqb + kr[s], kcol)),
                  pl.BlockSpec((t, D_DSA), lambda bb, s, qr, kr: (bb * nqb + kr[s], vcol)),
                  pl.BlockSpec((t, t), lambda bb, s, qr, kr: (bb * nqb + qr[s], kr[s])),
                  pl.BlockSpec((3, DSA_HEADS, t, t), lambda bb, s, qr, kr: (0, 0, 0, 0)),
                  pl.BlockSpec((DSA_HEADS, 1, LANES), lambda bb, s, qr, kr: (0, 0, 0))],
        out_specs=pl.BlockSpec((t, D_DSA), qrow),
        scratch_shapes=[pltpu.VMEM((DSA_HEADS, t, t), F32),
                        pltpu.VMEM((DSA_HEADS, t, LANES), F32), pltpu.VMEM((DSA_HEADS, t, LANES), F32),
                        pltpu.VMEM((t, D_DSA), F32)])
    return pl.pallas_call(
        functools.partial(_dsa_attn_kernel, lp=lp),
        grid_spec=grid_spec,
        out_shape=jax.ShapeDtypeStruct((n, D_DSA), BF16),
        compiler_params=_params(("parallel", "arbitrary")),
        name="dsa_attention",
    )(qb_of, kt_of, big, big, big, mask, band, far)


def _router_kernel(h_ref, w_ref, b_ref, info_ref, cnt_ref, carry_ref):
    i = pl.program_id(0)
    tm = h_ref.shape[0]

    @pl.when(i == 0)
    def _():
        carry_ref[...] = jnp.zeros_like(carry_ref)

    xh, xm, xl = _split3(h_ref[...])
    w = w_ref[...]
    wh, wm, wl = _split3(w)
    logits = (_dot(xh, wh) + (_dot(xh, wm) + _dot(xm, wh)) + (_dot(xm, wm) + _dot(xl, wh) + _dot(xh, wl))
              + b_ref[...])
    lane = lax.broadcasted_iota(I32, (tm, LANES), 1)
    logits = jnp.where(lane < N_EXPERTS, logits, -jnp.inf)
    v1 = jnp.max(logits, axis=1, keepdims=True)
    e1 = jnp.min(jnp.where(logits == v1, lane, LANES), axis=1, keepdims=True)
    rest = jnp.where(lane == e1, -jnp.inf, logits)
    v2 = jnp.max(rest, axis=1, keepdims=True)
    e2 = jnp.min(jnp.where(rest == v2, lane, LANES), axis=1, keepdims=True)
    ex = jnp.exp(v2 - v1)
    g1 = 1.0 / (1.0 + ex)
    g2 = ex / (1.0 + ex)
    onehot = jnp.where((lane == e1) | (lane == e2), 1.0, 0.0)
    row = lax.broadcasted_iota(I32, (tm, tm), 0)
    col = lax.broadcasted_iota(I32, (tm, tm), 1)
    before = jnp.where(col < row, 1.0, 0.0).astype(BF16)
    rank = _dot(before, onehot.astype(BF16)) + carry_ref[...]
    r1 = jnp.sum(jnp.where(lane == e1, rank, 0.0), axis=1, keepdims=True)
    r2 = jnp.sum(jnp.where(lane == e2, rank, 0.0), axis=1, keepdims=True)
    carry_ref[...] += jnp.sum(onehot, axis=0, keepdims=True)
    cnt_ref[...] = carry_ref[...]
    info = jnp.where(lane == 0, e1.astype(F32), 0.0)
    info = jnp.where(lane == 1, e2.astype(F32), info)
    info = jnp.where(lane == 2, r1, info)
    info = jnp.where(lane == 3, r2, info)
    info = jnp.where(lane == 4, g1, info)
    info = jnp.where(lane == 5, g2, info)
    info_ref[...] = info


def _router(h, router_w, router_b, tm):
    n, d = h.shape
    wpad = jnp.zeros((d, LANES), F32).at[:, :N_EXPERTS].set(router_w.astype(F32))
    bpad = jnp.zeros((1, LANES), F32).at[0, :N_EXPERTS].set(router_b.astype(F32))
    return pl.pallas_call(
        _router_kernel,
        grid=(n // tm,),
        in_specs=[pl.BlockSpec((tm, d), lambda i: (i, 0)),
                  pl.BlockSpec((d, LANES), lambda i: (0, 0)),
                  pl.BlockSpec((1, LANES), lambda i: (0, 0))],
        out_specs=[pl.BlockSpec((tm, LANES), lambda i: (i, 0)), pl.BlockSpec((1, LANES), lambda i: (0, 0))],
        out_shape=[jax.ShapeDtypeStruct((n, LANES), F32), jax.ShapeDtypeStruct((1, LANES), F32)],
        scratch_shapes=[pltpu.VMEM((1, LANES), F32)],
        compiler_params=_params(("arbitrary",)),
        name="router",
    )(h, wpad, bpad)


def _expert_kernel(be_ref, tok_ref, nu_ref, x_ref, w1_ref, w3_ref, w2_ref, y_ref, rows_ref, xb_ref, acc_ref, sem,
                   *, nf, nblk):
    del be_ref
    i = pl.program_id(0)
    f = pl.program_id(1)

    def copy(blk, slot, r):
        src = x_ref.at[pl.ds(tok_ref[blk * MOE_BLK + r], 1)]
        return pltpu.make_async_copy(src, rows_ref.at[slot, pl.ds(r, 1)], sem.at[slot])

    def gather(blk, slot):
        def body(r, carry):
            copy(blk, slot, r).start()
            return carry
        lax.fori_loop(0, MOE_BLK, body, 0, unroll=8)

    def drain(slot):
        def body(r, carry):
            copy(0, slot, 0).wait()
            return carry
        lax.fori_loop(0, MOE_BLK, body, 0, unroll=8)

    @pl.when(f == 0)
    def _():
        slot = i % 2

        @pl.when(i == 0)
        def _():
            gather(0, 0)

        drain(slot)

        @pl.when(i + 1 < nblk)
        def _():
            gather(i + 1, 1 - slot)

        xb_ref[...] = rows_ref[slot].astype(BF16)
        acc_ref[...] = jnp.zeros_like(acc_ref)

    @pl.when(i < nu_ref[0])
    def _():
        x = xb_ref[...]
        a = _dot(x, w1_ref[...])
        c = _dot(x, w3_ref[...])
        mid = (a * jax.nn.sigmoid(a) * c).astype(BF16)
        acc_ref[...] += _dot(mid, w2_ref[...])

    @pl.when(f == nf - 1)
    def _():
        y_ref[...] = acc_ref[...]


def _experts(x, slot_tok, blk_expert, n_used, w1, w3, w2, tf):
    d = x.shape[1]
    cap = slot_tok.shape[0]
    nblk = cap // MOE_BLK
    nf = w1.shape[2] // tf
    wcol = lambda i, f, nu: jnp.where(i < nu[0], f, 0)
    grid_spec = pltpu.PrefetchScalarGridSpec(
        num_scalar_prefetch=3,
        grid=(nblk, nf),
        in_specs=[pl.BlockSpec(memory_space=pl.ANY),
                  pl.BlockSpec((None, d, tf), lambda i, f, be, tk, nu: (be[i], 0, wcol(i, f, nu))),
                  pl.BlockSpec((None, d, tf), lambda i, f, be, tk, nu: (be[i], 0, wcol(i, f, nu))),
                  pl.BlockSpec((None, tf, d), lambda i, f, be, tk, nu: (be[i], wcol(i, f, nu), 0))],
        out_specs=pl.BlockSpec((MOE_BLK, d), lambda i, f, be, tk, nu: (i, 0)),
        scratch_shapes=[pltpu.VMEM((2, MOE_BLK, d), F32), pltpu.VMEM((MOE_BLK, d), BF16),
                        pltpu.VMEM((MOE_BLK, d), F32), pltpu.SemaphoreType.DMA((2,))])
    return pl.pallas_call(
        functools.partial(_expert_kernel, nf=nf, nblk=nblk),
        grid_spec=grid_spec,
        out_shape=jax.ShapeDtypeStruct((cap, d), F32),
        compiler_params=_params(("arbitrary", "arbitrary")),
        name="moe_experts",
    )(blk_expert, slot_tok, n_used, x, w1.astype(BF16), w3.astype(BF16), w2.astype(BF16))


def _combine_ln_kernel(d1_ref, d2_ref, yb_ref, h_ref, info_ref, g_ref, b_ref, o_ref, y1_ref, y2_ref, sem):
    tm = h_ref.shape[0]
    base = pl.program_id(0) * tm

    def copy(dest, buf, r):
        return pltpu.make_async_copy(yb_ref.at[pl.ds(dest, 1)], buf.at[pl.ds(r, 1)], sem)

    def issue(r, carry):
        copy(d1_ref[base + r], y1_ref, r).start()
        copy(d2_ref[base + r], y2_ref, r).start()
        return carry

    def drain(r, carry):
        copy(0, y1_ref, 0).wait()
        copy(0, y2_ref, 0).wait()
        return carry

    lax.fori_loop(0, tm, issue, 0, unroll=8)
    lax.fori_loop(0, tm, drain, 0, unroll=8)
    info = info_ref[...]
    moe = info[:, 4:5] * y1_ref[...] + info[:, 5:6] * y2_ref[...]
    o_ref[...] = _layer_norm(ALPHA * h_ref[...] + moe, g_ref[...], b_ref[...])


def _combine_ln(yb, d1, d2, h, info, g, b, tm):
    n, d = h.shape
    grid_spec = pltpu.PrefetchScalarGridSpec(
        num_scalar_prefetch=2,
        grid=(n // tm,),
        in_specs=[pl.BlockSpec(memory_space=pl.ANY),
                  pl.BlockSpec((tm, d), lambda i, a, c: (i, 0)),
                  pl.BlockSpec((tm, LANES), lambda i, a, c: (i, 0)),
                  pl.BlockSpec((1, d), lambda i, a, c: (0, 0)),
                  pl.BlockSpec((1, d), lambda i, a, c: (0, 0))],
        out_specs=pl.BlockSpec((tm, d), lambda i, a, c: (i, 0)),
        scratch_shapes=[pltpu.VMEM((tm, d), F32), pltpu.VMEM((tm, d), F32), pltpu.SemaphoreType.DMA(())])
    return pl.pallas_call(
        _combine_ln_kernel,
        grid_spec=grid_spec,
        out_shape=jax.ShapeDtypeStruct((n, d), F32),
        compiler_params=_params(("arbitrary",)),
        name="moe_combine_ln",
    )(d1, d2, yb, h, info, g.reshape(1, d), b.reshape(1, d))


def _moe_ln(h, router_w, router_b, w1, w3, w2, g, b, tm):
    n, d = h.shape
    info, counts = _router(h, router_w, router_b, tm)
    counts = counts[0, :N_EXPERTS].astype(I32)
    padded = (counts + MOE_BLK - 1) // MOE_BLK * MOE_BLK
    pad_end = jnp.cumsum(padded)
    pad_start = pad_end - padded
    e1, e2 = info[:, 0].astype(I32), info[:, 1].astype(I32)
    d1 = pad_start[e1] + info[:, 2].astype(I32)
    d2 = pad_start[e2] + info[:, 3].astype(I32)
    nblk = -(-(n * TOP_K) // MOE_BLK) + N_EXPERTS
    blk_expert = jnp.minimum(jnp.searchsorted(pad_end, jnp.arange(nblk, dtype=I32) * MOE_BLK, side='right'),
                             N_EXPERTS - 1).astype(I32)
    tok = jnp.arange(n, dtype=I32)
    slot_tok = jnp.zeros((nblk * MOE_BLK,), I32).at[jnp.concatenate([d1, d2])].set(
        jnp.concatenate([tok, tok]), unique_indices=True)
    n_used = (pad_end[-1:] // MOE_BLK).astype(I32)
    yb = _experts(h, slot_tok, blk_expert, n_used, w1, w3, w2, 512)
    return _combine_ln(yb, d1, d2, h, info, g, b, tm)


def kernel(x, meta_tokens, w_in_even, conv_w, s5_a_re, s5_a_im, s5_log_dt, s5_b_re, s5_b_im, s5_c_re, s5_c_im, s5_d, s5_w_glu, w_out_even, ln_mix_even_g, ln_mix_even_b, ffn_w1, ffn_w3, ffn_w2, ln_ffn_even_g, ln_ffn_even_b, w_in_odd, mlstm_b_i, mlstm_b_f, rel_bias_table, w_out_odd, ln_mix_odd_g, ln_mix_odd_b, router_w, router_b, expert_w1, expert_w3, expert_w2, ln_ffn_odd_g, ln_ffn_odd_b):
    bsz, seq, d = x.shape
    topk = min(TOPK_MAX, seq // 4)
    lh = seq + N_META
    lp = -(-lh // SEQ_TILE) * SEQ_TILE
    n = bsz * lp
    tm = _pick(n, (1280, 640, 512, 256))
    tln = _pick(n, (512, 256))
    tseq = _pick(lp, (1280, 640, 256))

    h = jnp.concatenate([jnp.broadcast_to(meta_tokens.astype(x.dtype)[None], (bsz, N_META, d)), x,
                         jnp.zeros((bsz, lp - lh, d), x.dtype)], axis=1).reshape(n, d)
    hb = h.astype(BF16)

    p = _proj(hb, w_in_even[0].astype(BF16), F32, tm, 1280)
    y_conv = _conv_mixer(p, conv_w[0].astype(F32), lp, tseq, 512)
    ops = _s5_operators(s5_a_re[0], s5_a_im[0], s5_log_dt[0], s5_b_re[0], s5_b_im[0], s5_c_re[0], s5_c_im[0], S5_T)
    u = p[:, 3 * D_CONV:].astype(BF16)
    y_lin = _s5_linear(u, ops, bsz, lp)
    y_s5 = _s5_glu(y_lin, p, s5_d[0], s5_w_glu[0], tm)
    h, hb = _outproj_ln(y_conv, y_s5, w_out_even[0], h, ln_mix_even_g[0], ln_mix_even_b[0], tln)
    h, hb = _ffn_ln(h, ffn_w1[0], ffn_w3[0], ffn_w2[0], ln_ffn_even_g[0], ln_ffn_even_b[0], tln, 512)

    w = w_in_odd[0]
    o = np.cumsum([0] + [D_MLSTM] * 4 + [MLSTM_HEADS] * 2 + [D_DSA] * 3 + [IDX_HEADS * IDX_DH, IDX_DH, IDX_HEADS])
    seg = lambda j: w[:, o[j]:o[j + 1]]
    w_big = jnp.concatenate([seg(0), seg(1), seg(2), seg(3), seg(6), seg(7), seg(8), seg(9)], axis=1)
    w_small = jnp.concatenate([seg(4), seg(5), seg(10), seg(11)], axis=1)
    w_small = jnp.pad(w_small, ((0, 0), (0, LANES - w_small.shape[1])))
    big = _proj(hb, w_big.astype(BF16), BF16, tm, 1280)
    small = _proj(hb, w_small.astype(BF16), F32, tm, LANES)
    gate_bias = jnp.zeros((1, LANES), F32).at[0, :MLSTM_HEADS].set(mlstm_b_i[0].astype(F32))
    gate_bias = gate_bias.at[0, MLSTM_HEADS:2 * MLSTM_HEADS].set(mlstm_b_f[0].astype(F32))
    y_m = _mlstm(big, small, gate_bias, bsz, lp)
    qi = big[:, 7 * D_DSA:].reshape(bsz, lp, IDX_HEADS, IDX_DH).transpose(0, 2, 1, 3)
    kidx = small[:, 2 * MLSTM_HEADS:2 * MLSTM_HEADS + IDX_DH].astype(BF16).reshape(bsz, lp, IDX_DH).transpose(0, 2, 1)
    widx = small[:, 2 * MLSTM_HEADS + IDX_DH:2 * MLSTM_HEADS + IDX_DH + IDX_HEADS]
    mask = _dsa_select(qi, kidx, widx, bsz, lp, topk)
    y_d = _dsa_attention(big, mask, rel_bias_table, bsz, lp)
    h, hb = _outproj_ln(y_m, y_d, w_out_odd[0], h, ln_mix_odd_g[0], ln_mix_odd_b[0], tln)
    h = _moe_ln(h, router_w[0], router_b[0], expert_w1[0], expert_w3[0], expert_w2[0],
                ln_ffn_odd_g[0], ln_ffn_odd_b[0], tln)
    return h.reshape(bsz, lp, d)[:, N_META:lh]
```

```python
import functools
import math

import jax
import jax.numpy as jnp
import numpy as np
from jax import lax
from jax.experimental import pallas as pl
from jax.experimental.pallas import tpu as pltpu

F32 = jnp.float32
BF16 = jnp.bfloat16
I32 = jnp.int32
I16 = jnp.int16

D_MODEL = 2048
DEPTH = 2
CHUNK = 64
N_META = 16
CONV_WIDTH = 3
D_CONV = 1536
S5_CH = 16
S5_STATE = 64
D_SSM = 512
S5_GROUPS = D_SSM // S5_CH
MLSTM_HEADS = 4
MLSTM_DH = 256
D_MLSTM = MLSTM_HEADS * MLSTM_DH
DSA_HEADS = 8
DSA_DH = 128
D_DSA = DSA_HEADS * DSA_DH
IDX_HEADS = 8
IDX_DH = 64
TOPK_MAX = 256
NUM_BUCKETS = 32
MAX_DISTANCE = 128
D_FF = 5632
N_EXPERTS = 8
TOP_K = 2
D_EXPERT = 5632
ALPHA = (2 * DEPTH) ** 0.25
LN_EPS = 1e-5

LANES = 128
SEQ_TILE = 256
S5_T = 64
SEARCH_W = 1024
MOE_BLK = 512
VMEM_LIMIT = 56 * 1024 * 1024
NEG = -1e30
LOG2E = math.log2(math.e)
INT_MIN = -(2 ** 31)


def _params(sem, vmem=VMEM_LIMIT):
    return pltpu.CompilerParams(dimension_semantics=sem, vmem_limit_bytes=vmem)


def _pick(n, candidates):
    for c in candidates:
        if n % c == 0:
            return c
    raise ValueError(f"no tile in {candidates} divides {n}")


def _dot(a, b):
    return jnp.dot(a, b, preferred_element_type=F32)


def _dot_nt(a, b):
    return lax.dot_general(a, b, (((1,), (1,)), ((), ())), preferred_element_type=F32)


def _split3(x):
    hi = x.astype(BF16)
    r = x - hi.astype(F32)
    mid = r.astype(BF16)
    lo = (r - mid.astype(F32)).astype(BF16)
    return hi, mid, lo


def _layer_norm(z, g, b):
    mu = jnp.mean(z, axis=-1, keepdims=True)
    zc = z - mu
    var = jnp.mean(zc * zc, axis=-1, keepdims=True)
    return zc * lax.rsqrt(var + LN_EPS) * g + b


def _proj_kernel(x_ref, w_ref, o_ref):
    o_ref[...] = _dot(x_ref[...], w_ref[...]).astype(o_ref.dtype)


def _proj(x, w, out_dtype, tm, tn):
    n, k = x.shape
    m = w.shape[1]
    return pl.pallas_call(
        _proj_kernel,
        grid=(n // tm, m // tn),
        in_specs=[pl.BlockSpec((tm, k), lambda i, j: (i, 0)),
                  pl.BlockSpec((k, tn), lambda i, j: (0, j))],
        out_specs=pl.BlockSpec((tm, tn), lambda i, j: (i, j)),
        out_shape=jax.ShapeDtypeStruct((n, m), out_dtype),
        compiler_params=_params(("parallel", "parallel")),
        name="proj",
    )(x, w)


def _outproj_ln_kernel(ya_ref, yb_ref, wa_ref, wb_ref, h_ref, g_ref, b_ref, o_ref, ob_ref):
    mix = _dot(ya_ref[...], wa_ref[...]) + _dot(yb_ref[...], wb_ref[...])
    out = _layer_norm(ALPHA * h_ref[...] + mix, g_ref[...], b_ref[...])
    o_ref[...] = out
    ob_ref[...] = out.astype(BF16)


def _outproj_ln(ya, yb, w, h, g, b, tm):
    n, d = h.shape
    ka, kb = ya.shape[1], yb.shape[1]
    wa, wb = w[:ka].astype(BF16), w[ka:].astype(BF16)
    row = lambda i: (i, 0)
    fixed = lambda i: (0, 0)
    return pl.pallas_call(
        _outproj_ln_kernel,
        grid=(n // tm,),
        in_specs=[pl.BlockSpec((tm, ka), row), pl.BlockSpec((tm, kb), row),
                  pl.BlockSpec((ka, d), fixed), pl.BlockSpec((kb, d), fixed),
                  pl.BlockSpec((tm, d), row),
                  pl.BlockSpec((1, d), fixed), pl.BlockSpec((1, d), fixed)],
        out_specs=[pl.BlockSpec((tm, d), row), pl.BlockSpec((tm, d), row)],
        out_shape=[jax.ShapeDtypeStruct((n, d), F32), jax.ShapeDtypeStruct((n, d), BF16)],
        compiler_params=_params(("parallel",)),
        name="outproj_ln",
    )(ya, yb, wa, wb, h, g.reshape(1, d), b.reshape(1, d))


def _conv_kernel(b_ref, c_ref, x_ref, ch_ref, xh_ref, w_ref, y_ref, *, tiles_per_seq):
    i = pl.program_id(0)
    z = c_ref[...] * x_ref[...]
    zh = ch_ref[...] * xh_ref[...]
    zh = jnp.where(i % tiles_per_seq == 0, 0.0, zh)
    r = lax.broadcasted_iota(I32, z.shape, 0)
    z1 = jnp.where(r == 0, zh[7:8], pltpu.roll(z, 1, 0))
    z2 = jnp.where(r == 0, zh[6:7], jnp.where(r == 1, zh[7:8], pltpu.roll(z, 2, 0)))
    w = w_ref[...]
    conv = w[0:1] * z2 + w[1:2] * z1 + w[2:3] * z
    y_ref[...] = (b_ref[...] * conv).astype(y_ref.dtype)


def _conv_mixer(p, conv_w, lp, tm, tc):
    n = p.shape[0]
    ncb = D_CONV // tc
    hb = tm // 8
    body = functools.partial(_conv_kernel, tiles_per_seq=lp // tm)
    return pl.pallas_call(
        body,
        grid=(n // tm, ncb),
        in_specs=[pl.BlockSpec((tm, tc), lambda i, j: (i, j)),
                  pl.BlockSpec((tm, tc), lambda i, j: (i, ncb + j)),
                  pl.BlockSpec((tm, tc), lambda i, j: (i, 2 * ncb + j)),
                  pl.BlockSpec((8, tc), lambda i, j: (jnp.maximum(i * hb - 1, 0), ncb + j)),
                  pl.BlockSpec((8, tc), lambda i, j: (jnp.maximum(i * hb - 1, 0), 2 * ncb + j)),
                  pl.BlockSpec((CONV_WIDTH, tc), lambda i, j: (0, j))],
        out_specs=pl.BlockSpec((tm, tc), lambda i, j: (i, j)),
        out_shape=jax.ShapeDtypeStruct((n, D_CONV), BF16),
        compiler_params=_params(("parallel", "parallel")),
        name="conv_mixer",
    )(p, p, p, p, p, conv_w)


def _s5_operators(a_re, a_im, log_dt, b_re, b_im, c_re, c_im, t):
    a = lax.complex(a_re.astype(F32), a_im.astype(F32))
    dt = jnp.exp(log_dt.astype(F32))[:, None]
    a_bar = jnp.exp(dt * a)
    b_bar = ((a_bar - 1.0) / a)[:, :, None] * lax.complex(b_re.astype(F32), b_im.astype(F32))
    c = lax.complex(c_re.astype(F32), c_im.astype(F32))
    lag = jnp.arange(t + 1, dtype=F32)[:, None, None]
    pw = jnp.exp(lag * (dt * a)[None])
    kern = jnp.real(jnp.einsum('gcp,lgp,gpd->glcd', c, pw[:t], b_bar))
    lagged = jnp.concatenate([jnp.zeros_like(kern), kern], axis=1)
    blocks = jnp.stack([lagged[:, t - j:2 * t - j] for j in range(t)], axis=1)
    g = a.shape[0]
    toep = blocks.transpose(0, 1, 4, 2, 3).reshape(g, t * S5_CH, t * S5_CH)
    wc = pw[:t][::-1].transpose(1, 0, 2)[:, :, None, :] * b_bar.transpose(0, 2, 1)[:, None, :, :]
    wc = wc.reshape(g, t * S5_CH, S5_STATE)
    w_in = jnp.concatenate([jnp.real(wc), jnp.imag(wc)], axis=-1)
    cp = c.transpose(0, 2, 1)[:, :, None, :] * pw[1:t + 1].transpose(1, 2, 0)[:, :, :, None]
    cp = cp.reshape(g, S5_STATE, t * S5_CH)
    v_out = jnp.concatenate([jnp.real(cp), -jnp.imag(cp)], axis=1)
    a_t = jnp.stack([jnp.real(pw[t]), jnp.imag(pw[t])], axis=1)
    return toep.astype(BF16), w_in.astype(BF16), v_out.astype(BF16), a_t


def _s5_kernel(u_ref, toep_ref, win_ref, vout_ref, at_ref, y_ref, er_ref, ei_ref, hr_ref, hi_ref, *, nbatch, nc):
    u = u_ref[...]
    e = _dot(u, win_ref[...])
    er_ref[...] = e[:, :S5_STATE]
    ei_ref[...] = e[:, S5_STATE:]
    at = at_ref[...]
    ar, ai = at[0:1], at[1:2]

    def step(c, carry):
        out = []
        for b in range(nbatch):
            hr, hi = carry[2 * b], carry[2 * b + 1]
            row = b * nc + c
            hr_ref[pl.ds(row, 1), :] = hr
            hi_ref[pl.ds(row, 1), :] = hi
            er = er_ref[pl.ds(row, 1), :]
            ei = ei_ref[pl.ds(row, 1), :]
            out += [ar * hr - ai * hi + er, ar * hi + ai * hr + ei]
        return tuple(out)

    zero = jnp.zeros((1, S5_STATE), F32)
    lax.fori_loop(0, nc, step, (zero,) * (2 * nbatch))
    vout = vout_ref[...]
    y = _dot(u, toep_ref[...])
    for part, rows in ((hr_ref[...], vout[:S5_STATE]), (hi_ref[...], vout[S5_STATE:])):
        hi_, mid_, _ = _split3(part)
        y += _dot(hi_, rows) + _dot(mid_, rows)
    y_ref[...] = y


def _s5_linear(u, ops, nbatch, lp):
    toep, w_in, v_out, a_t = ops
    n = u.shape[0]
    t = S5_T
    nc = lp // t
    rows = nbatch * nc
    width = t * S5_CH
    ug = u.reshape(rows, t, S5_GROUPS, S5_CH).transpose(2, 0, 1, 3).reshape(S5_GROUPS, rows, width)
    grp = lambda g: (g, 0, 0)
    yg = pl.pallas_call(
        functools.partial(_s5_kernel, nbatch=nbatch, nc=nc),
        grid=(S5_GROUPS,),
        in_specs=[pl.BlockSpec((None, rows, width), grp),
                  pl.BlockSpec((None, width, width), grp),
                  pl.BlockSpec((None, width, 2 * S5_STATE), grp),
                  pl.BlockSpec((None, 2 * S5_STATE, width), grp),
                  pl.BlockSpec((None, 2, S5_STATE), grp)],
        out_specs=pl.BlockSpec((None, rows, width), grp),
        out_shape=jax.ShapeDtypeStruct((S5_GROUPS, rows, width), F32),
        scratch_shapes=[pltpu.VMEM((rows, S5_STATE), F32)] * 4,
        compiler_params=_params(("parallel",)),
        name="s5_linear",
    )(ug, toep, w_in, v_out, a_t)
    return yg.reshape(S5_GROUPS, rows, t, S5_CH).transpose(1, 2, 0, 3).reshape(n, D_SSM)


def _s5_glu_kernel(y_ref, u_ref, d_ref, w_ref, o_ref):
    y = y_ref[...] + d_ref[...] * u_ref[...]
    y = jax.nn.gelu(y)
    hi, mid, _ = _split3(y)
    w = w_ref[...]
    gate = _dot(hi, w) + _dot(mid, w)
    o_ref[...] = (y * jax.nn.sigmoid(gate)).astype(o_ref.dtype)


def _s5_glu(y_lin, p, d_skip, w_glu, tm):
    n = y_lin.shape[0]
    ucol = (3 * D_CONV) // D_SSM
    wbd = jnp.einsum('gcd,gh->gdhc', w_glu.astype(F32), jnp.eye(S5_GROUPS, dtype=F32)).reshape(D_SSM, D_SSM)
    return pl.pallas_call(
        _s5_glu_kernel,
        grid=(n // tm,),
        in_specs=[pl.BlockSpec((tm, D_SSM), lambda i: (i, 0)),
                  pl.BlockSpec((tm, D_SSM), lambda i: (i, ucol)),
                  pl.BlockSpec((1, D_SSM), lambda i: (0, 0)),
                  pl.BlockSpec((D_SSM, D_SSM), lambda i: (0, 0))],
        out_specs=pl.BlockSpec((tm, D_SSM), lambda i: (i, 0)),
        out_shape=jax.ShapeDtypeStruct((n, D_SSM), BF16),
        compiler_params=_params(("parallel",)),
        name="s5_glu",
    )(y_lin, p, d_skip.reshape(1, D_SSM).astype(F32), wbd.astype(BF16))


def _ffn_ln_kernel(h_ref, w1_ref, w3_ref, w2_ref, g_ref, b_ref, o_ref, ob_ref, xb_ref, acc_ref, *, nf):
    f = pl.program_id(1)

    @pl.when(f == 0)
    def _():
        xb_ref[...] = h_ref[...].astype(BF16)
        acc_ref[...] = jnp.zeros_like(acc_ref)

    x = xb_ref[...]
    a = _dot(x, w1_ref[...])
    c = _dot(x, w3_ref[...])
    mid = (a * jax.nn.sigmoid(a) * c).astype(BF16)
    acc_ref[...] += _dot(mid, w2_ref[...])

    @pl.when(f == nf - 1)
    def _():
        out = _layer_norm(ALPHA * h_ref[...] + acc_ref[...], g_ref[...], b_ref[...])
        o_ref[...] = out
        ob_ref[...] = out.astype(BF16)


def _ffn_ln(h, w1, w3, w2, g, b, tm, tf):
    n, d = h.shape
    nf = w1.shape[1] // tf
    return pl.pallas_call(
        functools.partial(_ffn_ln_kernel, nf=nf),
        grid=(n // tm, nf),
        in_specs=[pl.BlockSpec((tm, d), lambda i, f: (i, 0)),
                  pl.BlockSpec((d, tf), lambda i, f: (0, f)),
                  pl.BlockSpec((d, tf), lambda i, f: (0, f)),
                  pl.BlockSpec((tf, d), lambda i, f: (f, 0)),
                  pl.BlockSpec((1, d), lambda i, f: (0, 0)),
                  pl.BlockSpec((1, d), lambda i, f: (0, 0))],
        out_specs=[pl.BlockSpec((tm, d), lambda i, f: (i, 0)), pl.BlockSpec((tm, d), lambda i, f: (i, 0))],
        out_shape=[jax.ShapeDtypeStruct((n, d), F32), jax.ShapeDtypeStruct((n, d), BF16)],
        scratch_shapes=[pltpu.VMEM((tm, d), BF16), pltpu.VMEM((tm, d), F32)],
        compiler_params=_params(("parallel", "arbitrary")),
        name="ffn_ln",
    )(h, w1.astype(BF16), w3.astype(BF16), w2.astype(BF16), g.reshape(1, d), b.reshape(1, d))


def _log_sigmoid(x):
    return jnp.minimum(x, 0.0) - jnp.log(1.0 + jnp.exp(-jnp.abs(x)))


def _mlstm_kernel(q_ref, k_ref, v_ref, o_ref, g_ref, gb_ref, y_ref, ct_ref, n_ref, m_ref):
    c = pl.program_id(1)
    t = q_ref.shape[0]
    dh = MLSTM_DH

    @pl.when(c == 0)
    def _():
        ct_ref[...] = jnp.zeros_like(ct_ref)
        n_ref[...] = jnp.zeros_like(n_ref)
        m_ref[...] = jnp.zeros_like(m_ref)

    gates = g_ref[...] + gb_ref[...]
    lf = _log_sigmoid(gates)
    row = lax.broadcasted_iota(I32, (t, t), 0)
    col = lax.broadcasted_iota(I32, (t, t), 1)
    tril = col <= row
    tril_b = jnp.where(tril, 1.0, 0.0).astype(BF16)
    hi, mid, lo = _split3(lf)
    bcum = _dot(tril_b, hi) + _dot(tril_b, mid) + _dot(tril_b, lo)
    gates_t = gates.T
    bcum_t = bcum.T
    for h in range(MLSTM_HEADS):
        sl = slice(h * dh, (h + 1) * dh)
        fh = MLSTM_HEADS + h
        li_row, li_col = gates_t[h:h + 1, :], gates[:, h:h + 1]
        bc_row, bc_col = bcum_t[fh:fh + 1, :], bcum[:, fh:fh + 1]
        m_prev = m_ref[h:h + 1, 0:1]
        dmat = jnp.where(tril, bc_col - bc_row + li_row, -jnp.inf)
        m_inter = bc_col + m_prev
        m_t = jnp.maximum(m_inter, jnp.max(dmat, axis=1, keepdims=True))
        q = q_ref[:, sl]
        k = k_ref[:, sl] * (dh ** -0.5)
        v = v_ref[:, sl]
        s = _dot_nt(q, k) * jnp.exp(dmat - m_t)
        w_inter = jnp.exp(m_inter - m_t)
        ct = ct_ref[h]
        nvec = n_ref[h:h + 1, :]
        num = _dot(s.astype(BF16), v) + w_inter * _dot(q, ct.astype(BF16))
        den = jnp.sum(s, axis=1, keepdims=True) + w_inter * jnp.sum(q.astype(F32) * nvec, axis=1, keepdims=True)
        h_out = num / jnp.maximum(jnp.abs(den), jnp.exp(-m_t))
        y_ref[:, sl] = (jax.nn.sigmoid(o_ref[:, sl].astype(F32)) * h_out).astype(y_ref.dtype)
        b_last = bc_col[t - 1:t, :]
        m_new = jnp.maximum(b_last + m_prev, jnp.max(b_last - bc_row + li_row, axis=1, keepdims=True))
        decay = jnp.exp(b_last + m_prev - m_new)
        kw = k.astype(F32) * jnp.exp(b_last - bc_col + li_col - m_new)
        ct_ref[h] = decay * ct + _dot(kw.T.astype(BF16), v)
        n_ref[h:h + 1, :] = decay * nvec + jnp.sum(kw, axis=0, keepdims=True)
        m_ref[h:h + 1, :] = jnp.broadcast_to(m_new, (1, LANES))


def _mlstm(big, small, gate_bias, nbatch, lp):
    n = big.shape[0]
    t = SEQ_TILE
    nc = lp // t
    blk = lambda j: pl.BlockSpec((t, D_MLSTM), lambda b, c: (b * nc + c, j))
    return pl.pallas_call(
        _mlstm_kernel,
        grid=(nbatch, nc),
        in_specs=[blk(0), blk(1), blk(2), blk(3),
                  pl.BlockSpec((t, LANES), lambda b, c: (b * nc + c, 0)),
                  pl.BlockSpec((1, LANES), lambda b, c: (0, 0))],
        out_specs=pl.BlockSpec((t, D_MLSTM), lambda b, c: (b * nc + c, 0)),
        out_shape=jax.ShapeDtypeStruct((n, D_MLSTM), BF16),
        scratch_shapes=[pltpu.VMEM((MLSTM_HEADS, MLSTM_DH, MLSTM_DH), F32),
                        pltpu.VMEM((8, MLSTM_DH), F32),
                        pltpu.VMEM((8, LANES), F32)],
        compiler_params=_params(("parallel", "arbitrary")),
        name="mlstm",
    )(big, big, big, big, small, gate_bias)


def _chunk_end(pos):
    return jnp.where(pos < N_META, N_META, N_META + CHUNK + CHUNK * ((pos - N_META) >> 6))


def _t5_bucket(rel):
    half = NUM_BUCKETS // 2
    max_exact = half // 2
    ret = jnp.where(rel > 0, half, 0)
    n = jnp.abs(rel)
    nf = jnp.maximum(n, 1).astype(F32)
    large = max_exact + (jnp.log(nf / max_exact) / math.log(MAX_DISTANCE / max_exact) * (half - max_exact)).astype(I32)
    large = jnp.minimum(large, half - 1)
    return ret + jnp.where(n < max_exact, n, large)


def _f32_order_key(x):
    bits = lax.bitcast_convert_type(x, I32)
    return bits ^ ((bits >> 31) & 0x7FFFFFFF)


def _lanes(x, width):
    return jnp.concatenate([x] * (width // LANES), axis=1)


def _dsa_select_kernel(qi_ref, kidx_ref, w_ref, mask_ref, hi_ref, lo_ref, wb_ref, lim_ref, *, lp, topk):
    qb, sw = SEQ_TILE, SEARCH_W
    b = pl.program_id(1)
    nkt = jnp.minimum(b + 2, lp // qb)
    nst = (nkt + sw // qb - 1) // (sw // qb)
    wi = w_ref[...] * (IDX_HEADS ** -0.5) * (IDX_DH ** -0.5)
    for h in range(IDX_HEADS):
        wb_ref[h] = jnp.broadcast_to(wi[:, h:h + 1], (qb, LANES))
    end = _chunk_end(b * qb + lax.broadcasted_iota(I32, (qb, 1), 0))
    lane = lax.broadcasted_iota(I32, (qb, qb), 1)
    low16 = jnp.int16(-(2 ** 15))

    def store_keys(off, key):
        hi_ref[:, pl.ds(off, qb)] = (key >> 16).astype(I16)
        lo_ref[:, pl.ds(off, qb)] = key.astype(I16) ^ low16

    qi = qi_ref[...]
    qi_all = jnp.concatenate([qi[:, h * IDX_DH:(h + 1) * IDX_DH] for h in range(IDX_HEADS)], axis=0)

    def fill(kt, visible_only):
        off = pl.multiple_of(kt * qb, qb)
        s_all = _dot(qi_all, kidx_ref[:, pl.ds(off, qb)])
        score = None
        for h in range(IDX_HEADS):
            term = _lanes(wb_ref[h], qb) * jnp.maximum(s_all[h * qb:(h + 1) * qb], 0.0)
            score = term if score is None else score + term
        key = _f32_order_key(score)
        if visible_only:
            key = jnp.where(off + lane < end, key, INT_MIN)
        store_keys(off, key)

    def loop(lo, hi, body):
        lax.fori_loop(lo, hi, lambda i, c: (body(i), c)[1], 0)

    loop(0, jnp.minimum(b, nkt), lambda kt: fill(kt, False))
    loop(jnp.minimum(b, nkt), nkt, lambda kt: fill(kt, True))
    loop(nkt, nst * (sw // qb), lambda kt: store_keys(pl.multiple_of(kt * qb, qb), jnp.full((qb, qb), INT_MIN, I32)))

    one = jnp.ones((qb, sw), BF16)
    zero = jnp.zeros((qb, sw), BF16)
    ones_rhs = jnp.ones((LANES, LANES), BF16)

    def count(*preds):
        def body(st, accs):
            off = pl.multiple_of(st * sw, sw)
            hi, lo = hi_ref[:, pl.ds(off, sw)], lo_ref[:, pl.ds(off, sw)]
            out = []
            for pred, acc in zip(preds, accs):
                hit = pred(hi, lo, off)
                parts = [hit[:, j * LANES:(j + 1) * LANES] for j in range(sw // LANES)]
                while len(parts) > 1:
                    parts = [parts[j] + parts[j + 1] for j in range(0, len(parts), 2)]
                out.append(acc + parts[0])
            return tuple(out)
        accs = lax.fori_loop(0, nst, body, (jnp.zeros((qb, LANES), BF16),) * len(preds))
        return [_dot(acc, ones_rhs) for acc in accs]

    def as16(u):
        return _lanes((u - 2 ** 15).astype(I16), sw)

    def search(pred_ge, target):
        def rnd(i, u):
            cand = u | jnp.left_shift(jnp.int32(1), 15 - i)
            cnt, = count(pred_ge(as16(cand)))
            return jnp.where(cnt >= target, cand, u)
        return lax.fori_loop(0, 16, rnd, jnp.zeros((qb, LANES), I32))

    kf = jnp.float32(topk)
    t_hi = search(lambda c: lambda hi, lo, off: jnp.where(hi >= c, one, zero), kf)
    thi16 = as16(t_hi)
    above_hi, = count(lambda hi, lo, off: jnp.where(hi > thi16, one, zero))

    def restrict(st):
        off = pl.multiple_of(st * sw, sw)
        lo_ref[:, pl.ds(off, sw)] = jnp.where(hi_ref[:, pl.ds(off, sw)] == thi16, lo_ref[:, pl.ds(off, sw)], low16)

    loop(0, nst, restrict)
    t_lo = search(lambda c: lambda hi, lo, off: jnp.where(lo >= c, one, zero), kf - above_hi)
    tlo16 = as16(t_lo)
    above_lo, equal = count(lambda hi, lo, off: jnp.where(lo > tlo16, one, zero),
                            lambda hi, lo, off: jnp.where(hi == thi16, jnp.where(lo == tlo16, one, zero), zero))
    above = above_hi + above_lo
    lowest = (t_hi == 0) & (t_lo == 0)
    tie = (above + equal > kf) & jnp.logical_not(lowest)
    lim_ref[...] = jnp.full((qb, LANES), lp, I32)

    def col16(off, width):
        return (off + lax.broadcasted_iota(I32, (qb, width), 1)).astype(I16)

    @pl.when(jnp.max(jnp.where(tie, 1, 0)) > 0)
    def _():
        need = kf - above

        def rnd(i, x):
            cand = x | jnp.left_shift(jnp.int32(1), 14 - i)
            c16 = _lanes(cand.astype(I16), sw)
            cnt, = count(lambda hi, lo, off: jnp.where(
                hi == thi16, jnp.where(lo == tlo16, jnp.where(col16(off, sw) < c16, one, zero), zero), zero))
            return jnp.where(cnt < need, cand, x)

        x = lax.fori_loop(0, 15, rnd, jnp.zeros((qb, LANES), I32))
        lim_ref[...] = jnp.where(tie, x, lp)

    thi_t, tlo_t, lim_t = (_lanes(v.astype(I16), qb) for v in (t_hi - 2 ** 15, t_lo - 2 ** 15, lim_ref[...]))
    keep = jnp.zeros((qb, qb), BF16)
    drop = jnp.full((qb, qb), NEG, BF16)

    def emit(kt):
        off = pl.multiple_of(kt * qb, qb)
        hi = hi_ref[:, pl.ds(off, qb)]
        lo = lo_ref[:, pl.ds(off, qb)]
        at_thr = jnp.where(lo == tlo_t, jnp.where(col16(off, qb) <= lim_t, keep, drop), drop)
        in_lo = jnp.where(lo > tlo_t, keep, at_thr)
        mask_ref[:, pl.ds(off, qb)] = jnp.where(hi > thi_t, keep, jnp.where(hi == thi_t, in_lo, drop))

    def blank(kt):
        mask_ref[:, pl.ds(pl.multiple_of(kt * qb, qb), qb)] = drop

    loop(0, nkt, emit)
    loop(nkt, lp // qb, blank)


def _dsa_select(big, kidx, w, nbatch, lp, topk):
    qb = SEQ_TILE
    nqb = lp // qb
    qi_col = big.shape[1] // (IDX_HEADS * IDX_DH) - 1
    lpad = -(-lp // SEARCH_W) * SEARCH_W
    assert lpad <= 256 * LANES
    return pl.pallas_call(
        functools.partial(_dsa_select_kernel, lp=lp, topk=topk),
        grid=(nbatch, nqb),
        in_specs=[pl.BlockSpec((qb, IDX_HEADS * IDX_DH), lambda bb, b: (bb * nqb + b, qi_col)),
                  pl.BlockSpec((None, IDX_DH, lp), lambda bb, b: (bb, 0, 0)),
                  pl.BlockSpec((qb, IDX_HEADS), lambda bb, b: (bb * nqb + b, 0))],
        out_specs=pl.BlockSpec((qb, lp), lambda bb, b: (bb * nqb + b, 0)),
        out_shape=jax.ShapeDtypeStruct((nbatch * lp, lp), BF16),
        scratch_shapes=[pltpu.VMEM((qb, lpad), I16), pltpu.VMEM((qb, lpad), I16),
                        pltpu.VMEM((IDX_HEADS, qb, LANES), F32), pltpu.VMEM((qb, LANES), I32)],
        compiler_params=_params(("parallel", "parallel")),
        name="dsa_select",
    )(big, kidx, w)


def _dsa_attn_kernel(qb_ref, kt_ref, q_ref, k_ref, v_ref, mask_ref, band_ref, far_ref, y_ref,
                     s_ref, m_ref, l_ref, acc_ref, *, lp):
    t = SEQ_TILE
    step = pl.program_id(1)
    b = qb_ref[step]
    kt = kt_ref[step]
    last_kt = jnp.minimum(b + 1, lp // t - 1)

    @pl.when(kt == 0)
    def _():
        m_ref[...] = jnp.full_like(m_ref, NEG)
        l_ref[...] = jnp.zeros_like(l_ref)
        acc_ref[...] = jnp.zeros_like(acc_ref)

    heads = [slice(h * DSA_DH, (h + 1) * DSA_DH) for h in range(DSA_HEADS)]
    near = kt >= b - 1
    mask = mask_ref[...].astype(F32)
    scale = DSA_DH ** -0.5 * LOG2E
    for h, sl in enumerate(heads):
        offset = mask + jnp.where(near, 0.0, far_ref[h])
        s_ref[h] = _dot_nt(q_ref[:, sl], k_ref[:, sl]) * scale + offset

    ones_v = jnp.ones((t, LANES), BF16)

    def update(z):
        m_prev = m_ref[...]
        m_next = jnp.maximum(m_prev, jnp.max(z, axis=2, keepdims=True))
        p = jnp.exp2(z - jnp.concatenate([m_next] * (t // LANES), axis=2)).astype(BF16)
        alpha = jnp.exp2(m_prev - m_next)
        m_ref[...] = m_next
        for h, sl in enumerate(heads):
            pv = _dot(p[h], jnp.concatenate([v_ref[:, sl], ones_v], axis=1))
            acc_ref[:, sl] = alpha[h] * acc_ref[:, sl] + pv[:, :DSA_DH]
            l_ref[h] = alpha[h] * l_ref[h] + pv[:, DSA_DH:]

    @pl.when(near)
    def _():
        update(s_ref[...] + band_ref[kt - b + 1])

    @pl.when(jnp.logical_not(near))
    def _():
        update(s_ref[...])

    @pl.when(kt == last_kt)
    def _():
        for h, sl in enumerate(heads):
            y_ref[:, sl] = (acc_ref[:, sl] / l_ref[h]).astype(y_ref.dtype)


def _dsa_bias_band(rel_bias_table):
    t = SEQ_TILE
    rel = jnp.arange(-2 * t + 1, 2 * t, dtype=I32)
    by_rel = (rel_bias_table.astype(F32)[_t5_bucket(rel)] * LOG2E).T
    bias = jnp.stack([by_rel[:, t - 1 - i:4 * t - 1 - i] for i in range(t)], axis=1)
    i = jnp.arange(t, dtype=I32)[:, None]
    c = jnp.arange(3 * t, dtype=I32)[None, :] - t
    band = jnp.where((c < _chunk_end(i))[None], bias, NEG)
    band = band.reshape(DSA_HEADS, t, 3, t).transpose(2, 0, 1, 3)
    far = rel_bias_table.astype(F32)[_t5_bucket(jnp.int32(-(t + 1)))] * LOG2E
    return band, far


def _dsa_attention(big, mask, rel_bias_table, nbatch, lp):
    t = SEQ_TILE
    nqb = lp // t
    n = big.shape[0]
    pairs = [(b, kt) for b in range(nqb) for kt in range(min(b + 2, nqb))]
    qb_of = jnp.asarray(np.array([p[0] for p in pairs], np.int32))
    kt_of = jnp.asarray(np.array([p[1] for p in pairs], np.int32))
    band, far = _dsa_bias_band(rel_bias_table)
    qcol, kcol, vcol = 4, 5, 6
    qrow = lambda bb, s, qr, kr: (bb * nqb + qr[s], 0)
    grid_spec = pltpu.PrefetchScalarGridSpec(
        num_scalar_prefetch=2,
        grid=(nbatch, len(pairs)),
        in_specs=[pl.BlockSpec((t, D_DSA), lambda bb, s, qr, kr: (bb * nqb + qr[s], qcol)),
                  pl.BlockSpec((t, D_DSA), lambda bb, s, qr, kr: (bb * nqb + kr[s], kcol)),
                  pl.BlockSpec((t, D_DSA), lambda bb, s, qr, kr: (bb * nqb + kr[s], vcol)),
                  pl.BlockSpec((t, t), lambda bb, s, qr, kr: (bb * nqb + qr[s], kr[s])),
                  pl.BlockSpec((3, DSA_HEADS, t, t), lambda bb, s, qr, kr: (0, 0, 0, 0)),
                  pl.BlockSpec(memory_space=pltpu.SMEM)],
        out_specs=pl.BlockSpec((t, D_DSA), qrow),
        scratch_shapes=[pltpu.VMEM((DSA_HEADS, t, t), F32),
                        pltpu.VMEM((DSA_HEADS, t, LANES), F32), pltpu.VMEM((DSA_HEADS, t, LANES), F32),
                        pltpu.VMEM((t, D_DSA), F32)])
    return pl.pallas_call(
        functools.partial(_dsa_attn_kernel, lp=lp),
        grid_spec=grid_spec,
        out_shape=jax.ShapeDtypeStruct((n, D_DSA), BF16),
        compiler_params=_params(("parallel", "arbitrary")),
        name="dsa_attention",
    )(qb_of, kt_of, big, big, big, mask, band, far)


def _router_kernel(h_ref, w_ref, b_ref, info_ref, cnt_ref, carry_ref):
    i = pl.program_id(0)
    tm = h_ref.shape[0]

    @pl.when(i == 0)
    def _():
        carry_ref[...] = jnp.zeros_like(carry_ref)

    xh, xm, _ = _split3(h_ref[...])
    w = w_ref[...]
    wh, wm, _ = _split3(w)
    logits = _dot(xh, wh) + (_dot(xh, wm) + _dot(xm, wh)) + b_ref[...]
    lane = lax.broadcasted_iota(I32, (tm, LANES), 1)
    logits = jnp.where(lane < N_EXPERTS, logits, -jnp.inf)
    v1 = jnp.max(logits, axis=1, keepdims=True)
    e1 = jnp.min(jnp.where(logits == v1, lane, LANES), axis=1, keepdims=True)
    rest = jnp.where(lane == e1, -jnp.inf, logits)
    v2 = jnp.max(rest, axis=1, keepdims=True)
    e2 = jnp.min(jnp.where(rest == v2, lane, LANES), axis=1, keepdims=True)
    ex = jnp.exp(v2 - v1)
    g1 = 1.0 / (1.0 + ex)
    g2 = ex / (1.0 + ex)
    onehot = jnp.where((lane == e1) | (lane == e2), 1.0, 0.0)
    row = lax.broadcasted_iota(I32, (tm, tm), 0)
    col = lax.broadcasted_iota(I32, (tm, tm), 1)
    before = jnp.where(col < row, 1.0, 0.0).astype(BF16)
    rank = _dot(before, onehot.astype(BF16)) + carry_ref[...]
    r1 = jnp.sum(jnp.where(lane == e1, rank, 0.0), axis=1, keepdims=True)
    r2 = jnp.sum(jnp.where(lane == e2, rank, 0.0), axis=1, keepdims=True)
    carry_ref[...] += jnp.sum(onehot, axis=0, keepdims=True)
    cnt_ref[...] = carry_ref[...]
    info = jnp.where(lane == 0, e1.astype(F32), 0.0)
    info = jnp.where(lane == 1, e2.astype(F32), info)
    info = jnp.where(lane == 2, r1, info)
    info = jnp.where(lane == 3, r2, info)
    info = jnp.where(lane == 4, g1, info)
    info = jnp.where(lane == 5, g2, info)
    info_ref[...] = info


def _router(h, router_w, router_b, tm):
    n, d = h.shape
    wpad = jnp.zeros((d, LANES), F32).at[:, :N_EXPERTS].set(router_w.astype(F32))
    bpad = jnp.zeros((1, LANES), F32).at[0, :N_EXPERTS].set(router_b.astype(F32))
    return pl.pallas_call(
        _router_kernel,
        grid=(n // tm,),
        in_specs=[pl.BlockSpec((tm, d), lambda i: (i, 0)),
                  pl.BlockSpec((d, LANES), lambda i: (0, 0)),
                  pl.BlockSpec((1, LANES), lambda i: (0, 0))],
        out_specs=[pl.BlockSpec((tm, LANES), lambda i: (i, 0)), pl.BlockSpec((1, LANES), lambda i: (0, 0))],
        out_shape=[jax.ShapeDtypeStruct((n, LANES), F32), jax.ShapeDtypeStruct((1, LANES), F32)],
        scratch_shapes=[pltpu.VMEM((1, LANES), F32)],
        compiler_params=_params(("arbitrary",)),
        name="router",
    )(h, wpad, bpad)


def _expert_kernel(be_ref, tok_ref, nu_ref, x_ref, w1_ref, w3_ref, w2_ref, y_ref, rows_ref, xb_ref, acc_ref, sem,
                   *, nf, nblk):
    del be_ref
    i = pl.program_id(0)
    f = pl.program_id(1)

    def copy(blk, slot, r):
        src = x_ref.at[pl.ds(tok_ref[blk * MOE_BLK + r], 1)]
        return pltpu.make_async_copy(src, rows_ref.at[slot, pl.ds(r, 1)], sem.at[slot])

    def gather(blk, slot):
        def body(r, carry):
            copy(blk, slot, r).start()
            return carry
        lax.fori_loop(0, MOE_BLK, body, 0, unroll=8)

    def drain(slot):
        def body(r, carry):
            copy(0, slot, 0).wait()
            return carry
        lax.fori_loop(0, MOE_BLK, body, 0, unroll=8)

    @pl.when(f == 0)
    def _():
        slot = i % 2

        @pl.when(i == 0)
        def _():
            gather(0, 0)

        drain(slot)

        @pl.when(i + 1 < nblk)
        def _():
            gather(i + 1, 1 - slot)

        xb_ref[...] = rows_ref[slot].astype(BF16)
        acc_ref[...] = jnp.zeros_like(acc_ref)

    @pl.when(i < nu_ref[0])
    def _():
        x = xb_ref[...]
        a = _dot(x, w1_ref[...])
        c = _dot(x, w3_ref[...])
        mid = (a * jax.nn.sigmoid(a) * c).astype(BF16)
        acc_ref[...] += _dot(mid, w2_ref[...])

    @pl.when(f == nf - 1)
    def _():
        y_ref[...] = acc_ref[...]


def _experts(x, slot_tok, blk_expert, n_used, w1, w3, w2, tf):
    d = x.shape[1]
    cap = slot_tok.shape[0]
    nblk = cap // MOE_BLK
    nf = w1.shape[2] // tf
    wcol = lambda i, f, nu: jnp.where(i < nu[0], f, 0)
    grid_spec = pltpu.PrefetchScalarGridSpec(
        num_scalar_prefetch=3,
        grid=(nblk, nf),
        in_specs=[pl.BlockSpec(memory_space=pl.ANY),
                  pl.BlockSpec((None, d, tf), lambda i, f, be, tk, nu: (be[i], 0, wcol(i, f, nu))),
                  pl.BlockSpec((None, d, tf), lambda i, f, be, tk, nu: (be[i], 0, wcol(i, f, nu))),
                  pl.BlockSpec((None, tf, d), lambda i, f, be, tk, nu: (be[i], wcol(i, f, nu), 0))],
        out_specs=pl.BlockSpec((MOE_BLK, d), lambda i, f, be, tk, nu: (i, 0)),
        scratch_shapes=[pltpu.VMEM((2, MOE_BLK, d), F32), pltpu.VMEM((MOE_BLK, d), BF16),
                        pltpu.VMEM((MOE_BLK, d), F32), pltpu.SemaphoreType.DMA((2,))])
    return pl.pallas_call(
        functools.partial(_expert_kernel, nf=nf, nblk=nblk),
        grid_spec=grid_spec,
        out_shape=jax.ShapeDtypeStruct((cap, d), F32),
        compiler_params=_params(("arbitrary", "arbitrary")),
        name="moe_experts",
    )(blk_expert, slot_tok, n_used, x, w1.astype(BF16), w3.astype(BF16), w2.astype(BF16))


def _combine_ln_kernel(d1_ref, d2_ref, yb_ref, h_ref, info_ref, g_ref, b_ref, o_ref, y1_ref, y2_ref, sem):
    tm = h_ref.shape[0]
    base = pl.program_id(0) * tm

    def copy(dest, buf, r):
        return pltpu.make_async_copy(yb_ref.at[pl.ds(dest, 1)], buf.at[pl.ds(r, 1)], sem)

    def issue(r, carry):
        copy(d1_ref[base + r], y1_ref, r).start()
        copy(d2_ref[base + r], y2_ref, r).start()
        return carry

    def drain(r, carry):
        copy(0, y1_ref, 0).wait()
        copy(0, y2_ref, 0).wait()
        return carry

    lax.fori_loop(0, tm, issue, 0, unroll=8)
    lax.fori_loop(0, tm, drain, 0, unroll=8)
    info = info_ref[...]
    moe = info[:, 4:5] * y1_ref[...] + info[:, 5:6] * y2_ref[...]
    o_ref[...] = _layer_norm(ALPHA * h_ref[...] + moe, g_ref[...], b_ref[...])


def _combine_ln(yb, d1, d2, h, info, g, b, tm):
    n, d = h.shape
    grid_spec = pltpu.PrefetchScalarGridSpec(
        num_scalar_prefetch=2,
        grid=(n // tm,),
        in_specs=[pl.BlockSpec(memory_space=pl.ANY),
                  pl.BlockSpec((tm, d), lambda i, a, c: (i, 0)),
                  pl.BlockSpec((tm, LANES), lambda i, a, c: (i, 0)),
                  pl.BlockSpec((1, d), lambda i, a, c: (0, 0)),
                  pl.BlockSpec((1, d), lambda i, a, c: (0, 0))],
        out_specs=pl.BlockSpec((tm, d), lambda i, a, c: (i, 0)),
        scratch_shapes=[pltpu.VMEM((tm, d), F32), pltpu.VMEM((tm, d), F32), pltpu.SemaphoreType.DMA(())])
    return pl.pallas_call(
        _combine_ln_kernel,
        grid_spec=grid_spec,
        out_shape=jax.ShapeDtypeStruct((n, d), F32),
        compiler_params=_params(("arbitrary",)),
        name="moe_combine_ln",
    )(d1, d2, yb, h, info, g.reshape(1, d), b.reshape(1, d))


def _moe_ln(h, router_w, router_b, w1, w3, w2, g, b, tm):
    n, d = h.shape
    info, counts = _router(h, router_w, router_b, tm)
    counts = counts[0, :N_EXPERTS].astype(I32)
    padded = (counts + MOE_BLK - 1) // MOE_BLK * MOE_BLK
    pad_end = jnp.cumsum(padded)
    pad_start = pad_end - padded
    e1, e2 = info[:, 0].astype(I32), info[:, 1].astype(I32)
    d1 = pad_start[e1] + info[:, 2].astype(I32)
    d2 = pad_start[e2] + info[:, 3].astype(I32)
    nblk = -(-(n * TOP_K) // MOE_BLK) + N_EXPERTS
    blk_expert = jnp.minimum(jnp.searchsorted(pad_end, jnp.arange(nblk, dtype=I32) * MOE_BLK, side='right'),
                             N_EXPERTS - 1).astype(I32)
    tok = jnp.arange(n, dtype=I32)
    slot_tok = jnp.zeros((nblk * MOE_BLK,), I32).at[jnp.concatenate([d1, d2])].set(
        jnp.concatenate([tok, tok]), unique_indices=True)
    n_used = (pad_end[-1:] // MOE_BLK).astype(I32)
    yb = _experts(h, slot_tok, blk_expert, n_used, w1, w3, w2, 512)
    return _combine_ln(yb, d1, d2, h, info, g, b, tm)


def kernel(x, meta_tokens, w_in_even, conv_w, s5_a_re, s5_a_im, s5_log_dt, s5_b_re, s5_b_im, s5_c_re, s5_c_im, s5_d, s5_w_glu, w_out_even, ln_mix_even_g, ln_mix_even_b, ffn_w1, ffn_w3, ffn_w2, ln_ffn_even_g, ln_ffn_even_b, w_in_odd, mlstm_b_i, mlstm_b_f, rel_bias_table, w_out_odd, ln_mix_odd_g, ln_mix_odd_b, router_w, router_b, expert_w1, expert_w3, expert_w2, ln_ffn_odd_g, ln_ffn_odd_b):
    bsz, seq, d = x.shape
    topk = min(TOPK_MAX, seq // 4)
    lh = seq + N_META
    lp = -(-lh // SEQ_TILE) * SEQ_TILE
    n = bsz * lp
    tm = _pick(n, (1280, 640, 512, 256))
    tln = _pick(n, (512, 256))
    tseq = _pick(lp, (1280, 640, 256))

    h = jnp.concatenate([jnp.broadcast_to(meta_tokens.astype(x.dtype)[None], (bsz, N_META, d)), x,
                         jnp.zeros((bsz, lp - lh, d), x.dtype)], axis=1).reshape(n, d)
    hb = h.astype(BF16)

    p = _proj(hb, w_in_even[0].astype(BF16), F32, tm, 1280)
    y_conv = _conv_mixer(p, conv_w[0].astype(F32), lp, tseq, 512)
    ops = _s5_operators(s5_a_re[0], s5_a_im[0], s5_log_dt[0], s5_b_re[0], s5_b_im[0], s5_c_re[0], s5_c_im[0], S5_T)
    u = p[:, 3 * D_CONV:].astype(BF16)
    y_lin = _s5_linear(u, ops, bsz, lp)
    y_s5 = _s5_glu(y_lin, p, s5_d[0], s5_w_glu[0], tm)
    h, hb = _outproj_ln(y_conv, y_s5, w_out_even[0], h, ln_mix_even_g[0], ln_mix_even_b[0], tln)
    h, hb = _ffn_ln(h, ffn_w1[0], ffn_w3[0], ffn_w2[0], ln_ffn_even_g[0], ln_ffn_even_b[0], tln, 512)

    w = w_in_odd[0]
    o = np.cumsum([0] + [D_MLSTM] * 4 + [MLSTM_HEADS] * 2 + [D_DSA] * 3 + [IDX_HEADS * IDX_DH, IDX_DH, IDX_HEADS])
    seg = lambda j: w[:, o[j]:o[j + 1]]
    w_big = jnp.concatenate([seg(0), seg(1), seg(2), seg(3), seg(6), seg(7), seg(8), seg(9)], axis=1)
    w_small = jnp.concatenate([seg(4), seg(5), seg(10), seg(11)], axis=1)
    w_small = jnp.pad(w_small, ((0, 0), (0, LANES - w_small.shape[1])))
    big = _proj(hb, w_big.astype(BF16), BF16, tm, 1280)
    small = _proj(hb, w_small.astype(BF16), F32, tm, LANES)
    gate_bias = jnp.zeros((1, LANES), F32).at[0, :MLSTM_HEADS].set(mlstm_b_i[0].astype(F32))
    gate_bias = gate_bias.at[0, MLSTM_HEADS:2 * MLSTM_HEADS].set(mlstm_b_f[0].astype(F32))
    y_m = _mlstm(big, small, gate_bias, bsz, lp)
    kidx = small[:, 2 * MLSTM_HEADS:2 * MLSTM_HEADS + IDX_DH].astype(BF16).reshape(bsz, lp, IDX_DH).transpose(0, 2, 1)
    widx = small[:, 2 * MLSTM_HEADS + IDX_DH:2 * MLSTM_HEADS + IDX_DH + IDX_HEADS]
    mask = _dsa_select(big, kidx, widx, bsz, lp, topk)
    y_d = _dsa_attention(big, mask, rel_bias_table, bsz, lp)
    h, hb = _outproj_ln(y_m, y_d, w_out_odd[0], h, ln_mix_odd_g[0], ln_mix_odd_b[0], tln)
    h = _moe_ln(h, router_w[0], router_b[0], expert_w1[0], expert_w3[0], expert_w2[0],
                ln_ffn_odd_g[0], ln_ffn_odd_b[0], tln)
    return h.reshape(bsz, lp, d)[:, N_META:lh]
```

```python
import functools
import math

import jax
import jax.numpy as jnp
import numpy as np
from jax import lax
from jax.experimental import pallas as pl
from jax.experimental.pallas import tpu as pltpu

F32 = jnp.float32
BF16 = jnp.bfloat16
I32 = jnp.int32
I16 = jnp.int16

D_MODEL = 2048
DEPTH = 2
CHUNK = 64
N_META = 16
CONV_WIDTH = 3
D_CONV = 1536
S5_CH = 16
S5_STATE = 64
D_SSM = 512
S5_GROUPS = D_SSM // S5_CH
MLSTM_HEADS = 4
MLSTM_DH = 256
D_MLSTM = MLSTM_HEADS * MLSTM_DH
DSA_HEADS = 8
DSA_DH = 128
D_DSA = DSA_HEADS * DSA_DH
IDX_HEADS = 8
IDX_DH = 64
TOPK_MAX = 256
NUM_BUCKETS = 32
MAX_DISTANCE = 128
D_FF = 5632
N_EXPERTS = 8
TOP_K = 2
D_EXPERT = 5632
ALPHA = (2 * DEPTH) ** 0.25
LN_EPS = 1e-5

LANES = 128
SEQ_TILE = 256
S5_T = 64
SEARCH_W = 2048
MOE_BLK = 512
VMEM_LIMIT = 56 * 1024 * 1024
NEG = -1e30
LOG2E = math.log2(math.e)
INT_MIN = -(2 ** 31)


def _params(sem, vmem=VMEM_LIMIT):
    return pltpu.CompilerParams(dimension_semantics=sem, vmem_limit_bytes=vmem)


def _pick(n, candidates):
    for c in candidates:
        if n % c == 0:
            return c
    raise ValueError(f"no tile in {candidates} divides {n}")


def _dot(a, b):
    return jnp.dot(a, b, preferred_element_type=F32)


def _dot_nt(a, b):
    return lax.dot_general(a, b, (((1,), (1,)), ((), ())), preferred_element_type=F32)


def _split3(x):
    hi = x.astype(BF16)
    r = x - hi.astype(F32)
    mid = r.astype(BF16)
    lo = (r - mid.astype(F32)).astype(BF16)
    return hi, mid, lo


def _layer_norm(z, g, b):
    mu = jnp.mean(z, axis=-1, keepdims=True)
    zc = z - mu
    var = jnp.mean(zc * zc, axis=-1, keepdims=True)
    return zc * lax.rsqrt(var + LN_EPS) * g + b


def _proj_kernel(x_ref, w_ref, o_ref):
    o_ref[...] = _dot(x_ref[...], w_ref[...]).astype(o_ref.dtype)


def _proj(x, w, out_dtype, tm, tn):
    n, k = x.shape
    m = w.shape[1]
    return pl.pallas_call(
        _proj_kernel,
        grid=(n // tm, m // tn),
        in_specs=[pl.BlockSpec((tm, k), lambda i, j: (i, 0)),
                  pl.BlockSpec((k, tn), lambda i, j: (0, j))],
        out_specs=pl.BlockSpec((tm, tn), lambda i, j: (i, j)),
        out_shape=jax.ShapeDtypeStruct((n, m), out_dtype),
        compiler_params=_params(("parallel", "parallel")),
        name="proj",
    )(x, w)


def _outproj_ln_kernel(ya_ref, yb_ref, wa_ref, wb_ref, h_ref, g_ref, b_ref, o_ref, ob_ref):
    mix = _dot(ya_ref[...], wa_ref[...]) + _dot(yb_ref[...], wb_ref[...])
    out = _layer_norm(ALPHA * h_ref[...] + mix, g_ref[...], b_ref[...])
    o_ref[...] = out
    ob_ref[...] = out.astype(BF16)


def _outproj_ln(ya, yb, w, h, g, b, tm):
    n, d = h.shape
    ka, kb = ya.shape[1], yb.shape[1]
    wa, wb = w[:ka].astype(BF16), w[ka:].astype(BF16)
    row = lambda i: (i, 0)
    fixed = lambda i: (0, 0)
    return pl.pallas_call(
        _outproj_ln_kernel,
        grid=(n // tm,),
        in_specs=[pl.BlockSpec((tm, ka), row), pl.BlockSpec((tm, kb), row),
                  pl.BlockSpec((ka, d), fixed), pl.BlockSpec((kb, d), fixed),
                  pl.BlockSpec((tm, d), row),
                  pl.BlockSpec((1, d), fixed), pl.BlockSpec((1, d), fixed)],
        out_specs=[pl.BlockSpec((tm, d), row), pl.BlockSpec((tm, d), row)],
        out_shape=[jax.ShapeDtypeStruct((n, d), F32), jax.ShapeDtypeStruct((n, d), BF16)],
        compiler_params=_params(("parallel",)),
        name="outproj_ln",
    )(ya, yb, wa, wb, h, g.reshape(1, d), b.reshape(1, d))


def _conv_kernel(b_ref, c_ref, x_ref, ch_ref, xh_ref, w_ref, y_ref, *, tiles_per_seq):
    i = pl.program_id(0)
    z = c_ref[...] * x_ref[...]
    zh = ch_ref[...] * xh_ref[...]
    zh = jnp.where(i % tiles_per_seq == 0, 0.0, zh)
    r = lax.broadcasted_iota(I32, z.shape, 0)
    z1 = jnp.where(r == 0, zh[7:8], pltpu.roll(z, 1, 0))
    z2 = jnp.where(r == 0, zh[6:7], jnp.where(r == 1, zh[7:8], pltpu.roll(z, 2, 0)))
    w = w_ref[...]
    conv = w[0:1] * z2 + w[1:2] * z1 + w[2:3] * z
    y_ref[...] = (b_ref[...] * conv).astype(y_ref.dtype)


def _conv_mixer(p, conv_w, lp, tm, tc):
    n = p.shape[0]
    ncb = D_CONV // tc
    hb = tm // 8
    body = functools.partial(_conv_kernel, tiles_per_seq=lp // tm)
    return pl.pallas_call(
        body,
        grid=(n // tm, ncb),
        in_specs=[pl.BlockSpec((tm, tc), lambda i, j: (i, j)),
                  pl.BlockSpec((tm, tc), lambda i, j: (i, ncb + j)),
                  pl.BlockSpec((tm, tc), lambda i, j: (i, 2 * ncb + j)),
                  pl.BlockSpec((8, tc), lambda i, j: (jnp.maximum(i * hb - 1, 0), ncb + j)),
                  pl.BlockSpec((8, tc), lambda i, j: (jnp.maximum(i * hb - 1, 0), 2 * ncb + j)),
                  pl.BlockSpec((CONV_WIDTH, tc), lambda i, j: (0, j))],
        out_specs=pl.BlockSpec((tm, tc), lambda i, j: (i, j)),
        out_shape=jax.ShapeDtypeStruct((n, D_CONV), BF16),
        compiler_params=_params(("parallel", "parallel")),
        name="conv_mixer",
    )(p, p, p, p, p, conv_w)


def _s5_operators(a_re, a_im, log_dt, b_re, b_im, c_re, c_im, t):
    a = lax.complex(a_re.astype(F32), a_im.astype(F32))
    dt = jnp.exp(log_dt.astype(F32))[:, None]
    a_bar = jnp.exp(dt * a)
    b_bar = ((a_bar - 1.0) / a)[:, :, None] * lax.complex(b_re.astype(F32), b_im.astype(F32))
    c = lax.complex(c_re.astype(F32), c_im.astype(F32))
    lag = jnp.arange(t + 1, dtype=F32)[:, None, None]
    pw = jnp.exp(lag * (dt * a)[None])
    kern = jnp.real(jnp.einsum('gcp,lgp,gpd->glcd', c, pw[:t], b_bar))
    lagged = jnp.concatenate([jnp.zeros_like(kern), kern], axis=1)
    blocks = jnp.stack([lagged[:, t - j:2 * t - j] for j in range(t)], axis=1)
    g = a.shape[0]
    toep = blocks.transpose(0, 1, 4, 2, 3).reshape(g, t * S5_CH, t * S5_CH)
    wc = pw[:t][::-1].transpose(1, 0, 2)[:, :, None, :] * b_bar.transpose(0, 2, 1)[:, None, :, :]
    wc = wc.reshape(g, t * S5_CH, S5_STATE)
    w_in = jnp.concatenate([jnp.real(wc), jnp.imag(wc)], axis=-1)
    cp = c.transpose(0, 2, 1)[:, :, None, :] * pw[1:t + 1].transpose(1, 2, 0)[:, :, :, None]
    cp = cp.reshape(g, S5_STATE, t * S5_CH)
    v_out = jnp.concatenate([jnp.real(cp), -jnp.imag(cp)], axis=1)
    a_t = jnp.stack([jnp.real(pw[t]), jnp.imag(pw[t])], axis=1)
    return toep.astype(BF16), w_in.astype(BF16), v_out.astype(BF16), a_t


def _s5_kernel(u_ref, toep_ref, win_ref, vout_ref, at_ref, y_ref, er_ref, ei_ref, hr_ref, hi_ref, *, nbatch, nc):
    u = u_ref[...]
    e = _dot(u, win_ref[...])
    er_ref[...] = e[:, :S5_STATE]
    ei_ref[...] = e[:, S5_STATE:]
    at = at_ref[...]
    ar, ai = at[0:1], at[1:2]

    def step(c, carry):
        out = []
        for b in range(nbatch):
            hr, hi = carry[2 * b], carry[2 * b + 1]
            row = b * nc + c
            hr_ref[pl.ds(row, 1), :] = hr
            hi_ref[pl.ds(row, 1), :] = hi
            er = er_ref[pl.ds(row, 1), :]
            ei = ei_ref[pl.ds(row, 1), :]
            out += [ar * hr - ai * hi + er, ar * hi + ai * hr + ei]
        return tuple(out)

    zero = jnp.zeros((1, S5_STATE), F32)
    lax.fori_loop(0, nc, step, (zero,) * (2 * nbatch))
    vout = vout_ref[...]
    y = _dot(u, toep_ref[...])
    for part, rows in ((hr_ref[...], vout[:S5_STATE]), (hi_ref[...], vout[S5_STATE:])):
        hi_, mid_, _ = _split3(part)
        y += _dot(hi_, rows) + _dot(mid_, rows)
    y_ref[...] = y


def _s5_linear(u, ops, nbatch, lp):
    toep, w_in, v_out, a_t = ops
    n = u.shape[0]
    t = S5_T
    nc = lp // t
    rows = nbatch * nc
    width = t * S5_CH
    ug = u.reshape(rows, t, S5_GROUPS, S5_CH).transpose(2, 0, 1, 3).reshape(S5_GROUPS, rows, width)
    grp = lambda g: (g, 0, 0)
    yg = pl.pallas_call(
        functools.partial(_s5_kernel, nbatch=nbatch, nc=nc),
        grid=(S5_GROUPS,),
        in_specs=[pl.BlockSpec((None, rows, width), grp),
                  pl.BlockSpec((None, width, width), grp),
                  pl.BlockSpec((None, width, 2 * S5_STATE), grp),
                  pl.BlockSpec((None, 2 * S5_STATE, width), grp),
                  pl.BlockSpec((None, 2, S5_STATE), grp)],
        out_specs=pl.BlockSpec((None, rows, width), grp),
        out_shape=jax.ShapeDtypeStruct((S5_GROUPS, rows, width), F32),
        scratch_shapes=[pltpu.VMEM((rows, S5_STATE), F32)] * 4,
        compiler_params=_params(("parallel",)),
        name="s5_linear",
    )(ug, toep, w_in, v_out, a_t)
    return yg.reshape(S5_GROUPS, rows, t, S5_CH).transpose(1, 2, 0, 3).reshape(n, D_SSM)


def _s5_glu_kernel(y_ref, u_ref, d_ref, w_ref, o_ref):
    y = y_ref[...] + d_ref[...] * u_ref[...]
    y = jax.nn.gelu(y)
    hi, mid, _ = _split3(y)
    w = w_ref[...]
    gate = _dot(hi, w) + _dot(mid, w)
    o_ref[...] = (y * jax.nn.sigmoid(gate)).astype(o_ref.dtype)


def _s5_glu(y_lin, p, d_skip, w_glu, tm):
    n = y_lin.shape[0]
    ucol = (3 * D_CONV) // D_SSM
    wbd = jnp.einsum('gcd,gh->gdhc', w_glu.astype(F32), jnp.eye(S5_GROUPS, dtype=F32)).reshape(D_SSM, D_SSM)
    return pl.pallas_call(
        _s5_glu_kernel,
        grid=(n // tm,),
        in_specs=[pl.BlockSpec((tm, D_SSM), lambda i: (i, 0)),
                  pl.BlockSpec((tm, D_SSM), lambda i: (i, ucol)),
                  pl.BlockSpec((1, D_SSM), lambda i: (0, 0)),
                  pl.BlockSpec((D_SSM, D_SSM), lambda i: (0, 0))],
        out_specs=pl.BlockSpec((tm, D_SSM), lambda i: (i, 0)),
        out_shape=jax.ShapeDtypeStruct((n, D_SSM), BF16),
        compiler_params=_params(("parallel",)),
        name="s5_glu",
    )(y_lin, p, d_skip.reshape(1, D_SSM).astype(F32), wbd.astype(BF16))


def _ffn_ln_kernel(h_ref, w1_ref, w3_ref, w2_ref, g_ref, b_ref, o_ref, ob_ref, xb_ref, acc_ref, *, nf):
    f = pl.program_id(1)

    @pl.when(f == 0)
    def _():
        xb_ref[...] = h_ref[...].astype(BF16)
        acc_ref[...] = jnp.zeros_like(acc_ref)

    x = xb_ref[...]
    a = _dot(x, w1_ref[...])
    c = _dot(x, w3_ref[...])
    mid = (a * jax.nn.sigmoid(a) * c).astype(BF16)
    acc_ref[...] += _dot(mid, w2_ref[...])

    @pl.when(f == nf - 1)
    def _():
        out = _layer_norm(ALPHA * h_ref[...] + acc_ref[...], g_ref[...], b_ref[...])
        o_ref[...] = out
        ob_ref[...] = out.astype(BF16)


def _ffn_ln(h, w1, w3, w2, g, b, tm, tf):
    n, d = h.shape
    nf = w1.shape[1] // tf
    return pl.pallas_call(
        functools.partial(_ffn_ln_kernel, nf=nf),
        grid=(n // tm, nf),
        in_specs=[pl.BlockSpec((tm, d), lambda i, f: (i, 0)),
                  pl.BlockSpec((d, tf), lambda i, f: (0, f)),
                  pl.BlockSpec((d, tf), lambda i, f: (0, f)),
                  pl.BlockSpec((tf, d), lambda i, f: (f, 0)),
                  pl.BlockSpec((1, d), lambda i, f: (0, 0)),
                  pl.BlockSpec((1, d), lambda i, f: (0, 0))],
        out_specs=[pl.BlockSpec((tm, d), lambda i, f: (i, 0)), pl.BlockSpec((tm, d), lambda i, f: (i, 0))],
        out_shape=[jax.ShapeDtypeStruct((n, d), F32), jax.ShapeDtypeStruct((n, d), BF16)],
        scratch_shapes=[pltpu.VMEM((tm, d), BF16), pltpu.VMEM((tm, d), F32)],
        compiler_params=_params(("parallel", "arbitrary")),
        name="ffn_ln",
    )(h, w1.astype(BF16), w3.astype(BF16), w2.astype(BF16), g.reshape(1, d), b.reshape(1, d))


def _log_sigmoid(x):
    return jnp.minimum(x, 0.0) - jnp.log(1.0 + jnp.exp(-jnp.abs(x)))


def _mlstm_kernel(q_ref, k_ref, v_ref, o_ref, g_ref, gb_ref, y_ref, ct_ref, n_ref, m_ref):
    c = pl.program_id(1)
    t = q_ref.shape[0]
    dh = MLSTM_DH

    @pl.when(c == 0)
    def _():
        ct_ref[...] = jnp.zeros_like(ct_ref)
        n_ref[...] = jnp.zeros_like(n_ref)
        m_ref[...] = jnp.zeros_like(m_ref)

    gates = g_ref[...] + gb_ref[...]
    lf = _log_sigmoid(gates)
    row = lax.broadcasted_iota(I32, (t, t), 0)
    col = lax.broadcasted_iota(I32, (t, t), 1)
    tril = col <= row
    tril_b = jnp.where(tril, 1.0, 0.0).astype(BF16)
    hi, mid, lo = _split3(lf)
    bcum = _dot(tril_b, hi) + _dot(tril_b, mid) + _dot(tril_b, lo)
    gates_t = gates.T
    bcum_t = bcum.T
    for h in range(MLSTM_HEADS):
        sl = slice(h * dh, (h + 1) * dh)
        fh = MLSTM_HEADS + h
        li_row, li_col = gates_t[h:h + 1, :], gates[:, h:h + 1]
        bc_row, bc_col = bcum_t[fh:fh + 1, :], bcum[:, fh:fh + 1]
        m_prev = m_ref[h:h + 1, 0:1]
        dmat = jnp.where(tril, bc_col - bc_row + li_row, -jnp.inf)
        m_inter = bc_col + m_prev
        m_t = jnp.maximum(m_inter, jnp.max(dmat, axis=1, keepdims=True))
        q = q_ref[:, sl]
        k = k_ref[:, sl] * (dh ** -0.5)
        v = v_ref[:, sl]
        s = _dot_nt(q, k) * jnp.exp(dmat - m_t)
        w_inter = jnp.exp(m_inter - m_t)
        ct = ct_ref[h]
        nvec = n_ref[h:h + 1, :]
        num = _dot(s.astype(BF16), v) + w_inter * _dot(q, ct.astype(BF16))
        den = jnp.sum(s, axis=1, keepdims=True) + w_inter * jnp.sum(q.astype(F32) * nvec, axis=1, keepdims=True)
        h_out = num / jnp.maximum(jnp.abs(den), jnp.exp(-m_t))
        y_ref[:, sl] = (jax.nn.sigmoid(o_ref[:, sl].astype(F32)) * h_out).astype(y_ref.dtype)
        b_last = bc_col[t - 1:t, :]
        m_new = jnp.maximum(b_last + m_prev, jnp.max(b_last - bc_row + li_row, axis=1, keepdims=True))
        decay = jnp.exp(b_last + m_prev - m_new)
        kw = k.astype(F32) * jnp.exp(b_last - bc_col + li_col - m_new)
        ct_ref[h] = decay * ct + _dot(kw.T.astype(BF16), v)
        n_ref[h:h + 1, :] = decay * nvec + jnp.sum(kw, axis=0, keepdims=True)
        m_ref[h:h + 1, :] = jnp.broadcast_to(m_new, (1, LANES))


def _mlstm(big, small, gate_bias, nbatch, lp):
    n = big.shape[0]
    t = SEQ_TILE
    nc = lp // t
    blk = lambda j: pl.BlockSpec((t, D_MLSTM), lambda b, c: (b * nc + c, j))
    return pl.pallas_call(
        _mlstm_kernel,
        grid=(nbatch, nc),
        in_specs=[blk(0), blk(1), blk(2), blk(3),
                  pl.BlockSpec((t, LANES), lambda b, c: (b * nc + c, 0)),
                  pl.BlockSpec((1, LANES), lambda b, c: (0, 0))],
        out_specs=pl.BlockSpec((t, D_MLSTM), lambda b, c: (b * nc + c, 0)),
        out_shape=jax.ShapeDtypeStruct((n, D_MLSTM), BF16),
        scratch_shapes=[pltpu.VMEM((MLSTM_HEADS, MLSTM_DH, MLSTM_DH), F32),
                        pltpu.VMEM((8, MLSTM_DH), F32),
                        pltpu.VMEM((8, LANES), F32)],
        compiler_params=_params(("parallel", "arbitrary")),
        name="mlstm",
    )(big, big, big, big, small, gate_bias)


def _chunk_end(pos):
    return jnp.where(pos < N_META, N_META, N_META + CHUNK + CHUNK * ((pos - N_META) >> 6))


def _t5_bucket(rel):
    half = NUM_BUCKETS // 2
    max_exact = half // 2
    ret = jnp.where(rel > 0, half, 0)
    n = jnp.abs(rel)
    nf = jnp.maximum(n, 1).astype(F32)
    large = max_exact + (jnp.log(nf / max_exact) / math.log(MAX_DISTANCE / max_exact) * (half - max_exact)).astype(I32)
    large = jnp.minimum(large, half - 1)
    return ret + jnp.where(n < max_exact, n, large)


def _f32_order_key(x):
    bits = lax.bitcast_convert_type(x, I32)
    return bits ^ ((bits >> 31) & 0x7FFFFFFF)


def _lanes(x, width):
    return jnp.concatenate([x] * (width // LANES), axis=1)


def _dsa_select_kernel(qi_ref, kidx_ref, w_ref, mask_ref, hi_ref, lo_ref, wb_ref, lim_ref, *, lp, topk):
    qb, sw = SEQ_TILE, SEARCH_W
    b = pl.program_id(1)
    nkt = jnp.minimum(b + 2, lp // qb)
    nst = (nkt + sw // qb - 1) // (sw // qb)
    wi = w_ref[...] * (IDX_HEADS ** -0.5) * (IDX_DH ** -0.5)
    for h in range(IDX_HEADS):
        wb_ref[h] = jnp.broadcast_to(wi[:, h:h + 1], (qb, LANES))
    end = _chunk_end(b * qb + lax.broadcasted_iota(I32, (qb, 1), 0))
    lane = lax.broadcasted_iota(I32, (qb, qb), 1)
    low16 = jnp.int16(-(2 ** 15))

    def store_keys(off, key):
        hi_ref[:, pl.ds(off, qb)] = (key >> 16).astype(I16)
        lo_ref[:, pl.ds(off, qb)] = key.astype(I16) ^ low16

    qi = qi_ref[...]
    qi_all = jnp.concatenate([qi[:, h * IDX_DH:(h + 1) * IDX_DH] for h in range(IDX_HEADS)], axis=0)

    def fill(kt, visible_only):
        off = pl.multiple_of(kt * qb, qb)
        s_all = _dot(qi_all, kidx_ref[:, pl.ds(off, qb)])
        score = None
        for h in range(IDX_HEADS):
            term = _lanes(wb_ref[h], qb) * jnp.maximum(s_all[h * qb:(h + 1) * qb], 0.0)
            score = term if score is None else score + term
        key = _f32_order_key(score)
        if visible_only:
            key = jnp.where(off + lane < end, key, INT_MIN)
        store_keys(off, key)

    def loop(lo, hi, body):
        lax.fori_loop(lo, hi, lambda i, c: (body(i), c)[1], 0)

    loop(0, jnp.minimum(b, nkt), lambda kt: fill(kt, False))
    loop(jnp.minimum(b, nkt), nkt, lambda kt: fill(kt, True))
    loop(nkt, nst * (sw // qb), lambda kt: store_keys(pl.multiple_of(kt * qb, qb), jnp.full((qb, qb), INT_MIN, I32)))

    one = jnp.ones((qb, sw), BF16)
    zero = jnp.zeros((qb, sw), BF16)
    ones_rhs = jnp.ones((LANES, LANES), BF16)

    def count(*preds):
        def body(st, accs):
            off = pl.multiple_of(st * sw, sw)
            hi, lo = hi_ref[:, pl.ds(off, sw)], lo_ref[:, pl.ds(off, sw)]
            out = []
            for pred, acc in zip(preds, accs):
                hit = pred(hi, lo, off)
                parts = [hit[:, j * LANES:(j + 1) * LANES] for j in range(sw // LANES)]
                while len(parts) > 1:
                    parts = [parts[j] + parts[j + 1] for j in range(0, len(parts), 2)]
                out.append(acc + parts[0])
            return tuple(out)
        accs = lax.fori_loop(0, nst, body, (jnp.zeros((qb, LANES), BF16),) * len(preds))
        return [_dot(acc, ones_rhs) for acc in accs]

    def as16(u):
        return _lanes((u - 2 ** 15).astype(I16), sw)

    def search(pred_ge, target):
        def rnd(i, u):
            cand = u | jnp.left_shift(jnp.int32(1), 15 - i)
            cnt, = count(pred_ge(as16(cand)))
            return jnp.where(cnt >= target, cand, u)
        return lax.fori_loop(0, 16, rnd, jnp.zeros((qb, LANES), I32))

    kf = jnp.float32(topk)
    t_hi = search(lambda c: lambda hi, lo, off: jnp.where(hi >= c, one, zero), kf)
    thi16 = as16(t_hi)
    above_hi, = count(lambda hi, lo, off: jnp.where(hi > thi16, one, zero))

    def restrict(st):
        off = pl.multiple_of(st * sw, sw)
        lo_ref[:, pl.ds(off, sw)] = jnp.where(hi_ref[:, pl.ds(off, sw)] == thi16, lo_ref[:, pl.ds(off, sw)], low16)

    loop(0, nst, restrict)
    t_lo = search(lambda c: lambda hi, lo, off: jnp.where(lo >= c, one, zero), kf - above_hi)
    tlo16 = as16(t_lo)
    above_lo, equal = count(lambda hi, lo, off: jnp.where(lo > tlo16, one, zero),
                            lambda hi, lo, off: jnp.where(hi == thi16, jnp.where(lo == tlo16, one, zero), zero))
    above = above_hi + above_lo
    lowest = (t_hi == 0) & (t_lo == 0)
    tie = (above + equal > kf) & jnp.logical_not(lowest)
    lim_ref[...] = jnp.full((qb, LANES), lp, I32)

    def col16(off, width):
        return (off + lax.broadcasted_iota(I32, (qb, width), 1)).astype(I16)

    @pl.when(jnp.max(jnp.where(tie, 1, 0)) > 0)
    def _():
        need = kf - above

        def rnd(i, x):
            cand = x | jnp.left_shift(jnp.int32(1), 14 - i)
            c16 = _lanes(cand.astype(I16), sw)
            cnt, = count(lambda hi, lo, off: jnp.where(
                hi == thi16, jnp.where(lo == tlo16, jnp.where(col16(off, sw) < c16, one, zero), zero), zero))
            return jnp.where(cnt < need, cand, x)

        x = lax.fori_loop(0, 15, rnd, jnp.zeros((qb, LANES), I32))
        lim_ref[...] = jnp.where(tie, x, lp)

    thi_t, tlo_t, lim_t = (_lanes(v.astype(I16), qb) for v in (t_hi - 2 ** 15, t_lo - 2 ** 15, lim_ref[...]))
    keep = jnp.zeros((qb, qb), BF16)
    drop = jnp.full((qb, qb), NEG, BF16)

    def emit(kt):
        off = pl.multiple_of(kt * qb, qb)
        hi = hi_ref[:, pl.ds(off, qb)]
        lo = lo_ref[:, pl.ds(off, qb)]
        at_thr = jnp.where(lo == tlo_t, jnp.where(col16(off, qb) <= lim_t, keep, drop), drop)
        in_lo = jnp.where(lo > tlo_t, keep, at_thr)
        mask_ref[:, pl.ds(off, qb)] = jnp.where(hi > thi_t, keep, jnp.where(hi == thi_t, in_lo, drop))

    def blank(kt):
        mask_ref[:, pl.ds(pl.multiple_of(kt * qb, qb), qb)] = drop

    loop(0, nkt, emit)
    loop(nkt, lp // qb, blank)


def _dsa_select(big, kidx, w, nbatch, lp, topk):
    qb = SEQ_TILE
    nqb = lp // qb
    qi_col = big.shape[1] // (IDX_HEADS * IDX_DH) - 1
    lpad = -(-lp // SEARCH_W) * SEARCH_W
    assert lpad <= 256 * LANES
    return pl.pallas_call(
        functools.partial(_dsa_select_kernel, lp=lp, topk=topk),
        grid=(nbatch, nqb),
        in_specs=[pl.BlockSpec((qb, IDX_HEADS * IDX_DH), lambda bb, b: (bb * nqb + b, qi_col)),
                  pl.BlockSpec((None, IDX_DH, lp), lambda bb, b: (bb, 0, 0)),
                  pl.BlockSpec((qb, IDX_HEADS), lambda bb, b: (bb * nqb + b, 0))],
        out_specs=pl.BlockSpec((qb, lp), lambda bb, b: (bb * nqb + b, 0)),
        out_shape=jax.ShapeDtypeStruct((nbatch * lp, lp), BF16),
        scratch_shapes=[pltpu.VMEM((qb, lpad), I16), pltpu.VMEM((qb, lpad), I16),
                        pltpu.VMEM((IDX_HEADS, qb, LANES), F32), pltpu.VMEM((qb, LANES), I32)],
        compiler_params=_params(("parallel", "parallel")),
        name="dsa_select",
    )(big, kidx, w)


def _dsa_attn_kernel(qb_ref, kt_ref, q_ref, k_ref, v_ref, mask_ref, band_ref, far_ref, y_ref,
                     s_ref, m_ref, l_ref, acc_ref, *, lp):
    t = SEQ_TILE
    step = pl.program_id(1)
    b = qb_ref[step]
    kt = kt_ref[step]
    last_kt = jnp.minimum(b + 1, lp // t - 1)

    @pl.when(kt == 0)
    def _():
        m_ref[...] = jnp.full_like(m_ref, NEG)
        l_ref[...] = jnp.zeros_like(l_ref)
        acc_ref[...] = jnp.zeros_like(acc_ref)

    heads = [slice(h * DSA_DH, (h + 1) * DSA_DH) for h in range(DSA_HEADS)]
    near = kt >= b - 1
    mask = mask_ref[...].astype(F32)
    scale = DSA_DH ** -0.5 * LOG2E
    for h, sl in enumerate(heads):
        offset = mask + jnp.where(near, 0.0, far_ref[h])
        s_ref[h] = _dot_nt(q_ref[:, sl], k_ref[:, sl]) * scale + offset

    ones_v = jnp.ones((t, LANES), BF16)

    def update(z):
        m_prev = m_ref[...]
        m_next = jnp.maximum(m_prev, jnp.max(z, axis=2, keepdims=True))
        p = jnp.exp2(z - jnp.concatenate([m_next] * (t // LANES), axis=2)).astype(BF16)
        alpha = jnp.exp2(m_prev - m_next)
        m_ref[...] = m_next
        for h, sl in enumerate(heads):
            pv = _dot(p[h], jnp.concatenate([v_ref[:, sl], ones_v], axis=1))
            acc_ref[:, sl] = alpha[h] * acc_ref[:, sl] + pv[:, :DSA_DH]
            l_ref[h] = alpha[h] * l_ref[h] + pv[:, DSA_DH:]

    @pl.when(near)
    def _():
        update(s_ref[...] + band_ref[kt - b + 1])

    @pl.when(jnp.logical_not(near))
    def _():
        update(s_ref[...])

    @pl.when(kt == last_kt)
    def _():
        for h, sl in enumerate(heads):
            y_ref[:, sl] = (acc_ref[:, sl] / l_ref[h]).astype(y_ref.dtype)


def _dsa_bias_band(rel_bias_table):
    t = SEQ_TILE
    rel = jnp.arange(-2 * t + 1, 2 * t, dtype=I32)
    by_rel = (rel_bias_table.astype(F32)[_t5_bucket(rel)] * LOG2E).T
    bias = jnp.stack([by_rel[:, t - 1 - i:4 * t - 1 - i] for i in range(t)], axis=1)
    i = jnp.arange(t, dtype=I32)[:, None]
    c = jnp.arange(3 * t, dtype=I32)[None, :] - t
    band = jnp.where((c < _chunk_end(i))[None], bias, NEG)
    band = band.reshape(DSA_HEADS, t, 3, t).transpose(2, 0, 1, 3)
    far = rel_bias_table.astype(F32)[_t5_bucket(jnp.int32(-(t + 1)))] * LOG2E
    return band, far


def _dsa_attention(big, mask, rel_bias_table, nbatch, lp):
    t = SEQ_TILE
    nqb = lp // t
    n = big.shape[0]
    pairs = [(b, kt) for b in range(nqb) for kt in range(min(b + 2, nqb))]
    qb_of = jnp.asarray(np.array([p[0] for p in pairs], np.int32))
    kt_of = jnp.asarray(np.array([p[1] for p in pairs], np.int32))
    band, far = _dsa_bias_band(rel_bias_table)
    qcol, kcol, vcol = 4, 5, 6
    qrow = lambda bb, s, qr, kr: (bb * nqb + qr[s], 0)
    grid_spec = pltpu.PrefetchScalarGridSpec(
        num_scalar_prefetch=2,
        grid=(nbatch, len(pairs)),
        in_specs=[pl.BlockSpec((t, D_DSA), lambda bb, s, qr, kr: (bb * nqb + qr[s], qcol)),
                  pl.BlockSpec((t, D_DSA), lambda bb, s, qr, kr: (bb * nqb + kr[s], kcol)),
                  pl.BlockSpec((t, D_DSA), lambda bb, s, qr, kr: (bb * nqb + kr[s], vcol)),
                  pl.BlockSpec((t, t), lambda bb, s, qr, kr: (bb * nqb + qr[s], kr[s])),
                  pl.BlockSpec((3, DSA_HEADS, t, t), lambda bb, s, qr, kr: (0, 0, 0, 0)),
                  pl.BlockSpec(memory_space=pltpu.SMEM)],
        out_specs=pl.BlockSpec((t, D_DSA), qrow),
        scratch_shapes=[pltpu.VMEM((DSA_HEADS, t, t), F32),
                        pltpu.VMEM((DSA_HEADS, t, LANES), F32), pltpu.VMEM((DSA_HEADS, t, LANES), F32),
                        pltpu.VMEM((t, D_DSA), F32)])
    return pl.pallas_call(
        functools.partial(_dsa_attn_kernel, lp=lp),
        grid_spec=grid_spec,
        out_shape=jax.ShapeDtypeStruct((n, D_DSA), BF16),
        compiler_params=_params(("parallel", "arbitrary")),
        name="dsa_attention",
    )(qb_of, kt_of, big, big, big, mask, band, far)


def _router_kernel(h_ref, w_ref, b_ref, info_ref, cnt_ref, carry_ref):
    i = pl.program_id(0)
    tm = h_ref.shape[0]

    @pl.when(i == 0)
    def _():
        carry_ref[...] = jnp.zeros_like(carry_ref)

    xh, xm, _ = _split3(h_ref[...])
    w = w_ref[...]
    wh, wm, _ = _split3(w)
    logits = _dot(xh, wh) + (_dot(xh, wm) + _dot(xm, wh)) + b_ref[...]
    lane = lax.broadcasted_iota(I32, (tm, LANES), 1)
    logits = jnp.where(lane < N_EXPERTS, logits, -jnp.inf)
    v1 = jnp.max(logits, axis=1, keepdims=True)
    e1 = jnp.min(jnp.where(logits == v1, lane, LANES), axis=1, keepdims=True)
    rest = jnp.where(lane == e1, -jnp.inf, logits)
    v2 = jnp.max(rest, axis=1, keepdims=True)
    e2 = jnp.min(jnp.where(rest == v2, lane, LANES), axis=1, keepdims=True)
    ex = jnp.exp(v2 - v1)
    g1 = 1.0 / (1.0 + ex)
    g2 = ex / (1.0 + ex)
    onehot = jnp.where((lane == e1) | (lane == e2), 1.0, 0.0)
    row = lax.broadcasted_iota(I32, (tm, tm), 0)
    col = lax.broadcasted_iota(I32, (tm, tm), 1)
    before = jnp.where(col < row, 1.0, 0.0).astype(BF16)
    rank = _dot(before, onehot.astype(BF16)) + carry_ref[...]
    r1 = jnp.sum(jnp.where(lane == e1, rank, 0.0), axis=1, keepdims=True)
    r2 = jnp.sum(jnp.where(lane == e2, rank, 0.0), axis=1, keepdims=True)
    carry_ref[...] += jnp.sum(onehot, axis=0, keepdims=True)
    cnt_ref[...] = carry_ref[...]
    info = jnp.where(lane == 0, e1.astype(F32), 0.0)
    info = jnp.where(lane == 1, e2.astype(F32), info)
    info = jnp.where(lane == 2, r1, info)
    info = jnp.where(lane == 3, r2, info)
    info = jnp.where(lane == 4, g1, info)
    info = jnp.where(lane == 5, g2, info)
    info_ref[...] = info


def _router(h, router_w, router_b, tm):
    n, d = h.shape
    wpad = jnp.zeros((d, LANES), F32).at[:, :N_EXPERTS].set(router_w.astype(F32))
    bpad = jnp.zeros((1, LANES), F32).at[0, :N_EXPERTS].set(router_b.astype(F32))
    return pl.pallas_call(
        _router_kernel,
        grid=(n // tm,),
        in_specs=[pl.BlockSpec((tm, d), lambda i: (i, 0)),
                  pl.BlockSpec((d, LANES), lambda i: (0, 0)),
                  pl.BlockSpec((1, LANES), lambda i: (0, 0))],
        out_specs=[pl.BlockSpec((tm, LANES), lambda i: (i, 0)), pl.BlockSpec((1, LANES), lambda i: (0, 0))],
        out_shape=[jax.ShapeDtypeStruct((n, LANES), F32), jax.ShapeDtypeStruct((1, LANES), F32)],
        scratch_shapes=[pltpu.VMEM((1, LANES), F32)],
        compiler_params=_params(("arbitrary",)),
        name="router",
    )(h, wpad, bpad)


def _expert_kernel(be_ref, tok_ref, nu_ref, x_ref, w1_ref, w3_ref, w2_ref, y_ref, rows_ref, xb_ref, acc_ref, sem,
                   *, nf, nblk):
    del be_ref
    i = pl.program_id(0)
    f = pl.program_id(1)

    def copy(blk, slot, r):
        src = x_ref.at[pl.ds(tok_ref[blk * MOE_BLK + r], 1)]
        return pltpu.make_async_copy(src, rows_ref.at[slot, pl.ds(r, 1)], sem.at[slot])

    def gather(blk, slot):
        def body(r, carry):
            copy(blk, slot, r).start()
            return carry
        lax.fori_loop(0, MOE_BLK, body, 0, unroll=8)

    def drain(slot):
        def body(r, carry):
            copy(0, slot, 0).wait()
            return carry
        lax.fori_loop(0, MOE_BLK, body, 0, unroll=8)

    @pl.when(f == 0)
    def _():
        slot = i % 2

        @pl.when(i == 0)
        def _():
            gather(0, 0)

        drain(slot)

        @pl.when(i + 1 < nblk)
        def _():
            gather(i + 1, 1 - slot)

        xb_ref[...] = rows_ref[slot].astype(BF16)
        acc_ref[...] = jnp.zeros_like(acc_ref)

    @pl.when(i < nu_ref[0])
    def _():
        x = xb_ref[...]
        a = _dot(x, w1_ref[...])
        c = _dot(x, w3_ref[...])
        mid = (a * jax.nn.sigmoid(a) * c).astype(BF16)
        acc_ref[...] += _dot(mid, w2_ref[...])

    @pl.when(f == nf - 1)
    def _():
        y_ref[...] = acc_ref[...]


def _experts(x, slot_tok, blk_expert, n_used, w1, w3, w2, tf):
    d = x.shape[1]
    cap = slot_tok.shape[0]
    nblk = cap // MOE_BLK
    nf = w1.shape[2] // tf
    wcol = lambda i, f, nu: jnp.where(i < nu[0], f, 0)
    grid_spec = pltpu.PrefetchScalarGridSpec(
        num_scalar_prefetch=3,
        grid=(nblk, nf),
        in_specs=[pl.BlockSpec(memory_space=pl.ANY),
                  pl.BlockSpec((None, d, tf), lambda i, f, be, tk, nu: (be[i], 0, wcol(i, f, nu))),
                  pl.BlockSpec((None, d, tf), lambda i, f, be, tk, nu: (be[i], 0, wcol(i, f, nu))),
                  pl.BlockSpec((None, tf, d), lambda i, f, be, tk, nu: (be[i], wcol(i, f, nu), 0))],
        out_specs=pl.BlockSpec((MOE_BLK, d), lambda i, f, be, tk, nu: (i, 0)),
        scratch_shapes=[pltpu.VMEM((2, MOE_BLK, d), F32), pltpu.VMEM((MOE_BLK, d), BF16),
                        pltpu.VMEM((MOE_BLK, d), F32), pltpu.SemaphoreType.DMA((2,))])
    return pl.pallas_call(
        functools.partial(_expert_kernel, nf=nf, nblk=nblk),
        grid_spec=grid_spec,
        out_shape=jax.ShapeDtypeStruct((cap, d), F32),
        compiler_params=_params(("arbitrary", "arbitrary")),
        name="moe_experts",
    )(blk_expert, slot_tok, n_used, x, w1.astype(BF16), w3.astype(BF16), w2.astype(BF16))


def _combine_ln_kernel(d1_ref, d2_ref, yb_ref, h_ref, info_ref, g_ref, b_ref, o_ref, y1_ref, y2_ref, sem):
    tm = h_ref.shape[0]
    base = pl.program_id(0) * tm

    def copy(dest, buf, r):
        return pltpu.make_async_copy(yb_ref.at[pl.ds(dest, 1)], buf.at[pl.ds(r, 1)], sem)

    def issue(r, carry):
        copy(d1_ref[base + r], y1_ref, r).start()
        copy(d2_ref[base + r], y2_ref, r).start()
        return carry

    def drain(r, carry):
        copy(0, y1_ref, 0).wait()
        copy(0, y2_ref, 0).wait()
        return carry

    lax.fori_loop(0, tm, issue, 0, unroll=8)
    lax.fori_loop(0, tm, drain, 0, unroll=8)
    info = info_ref[...]
    moe = info[:, 4:5] * y1_ref[...] + info[:, 5:6] * y2_ref[...]
    o_ref[...] = _layer_norm(ALPHA * h_ref[...] + moe, g_ref[...], b_ref[...])


def _combine_ln(yb, d1, d2, h, info, g, b, tm):
    n, d = h.shape
    grid_spec = pltpu.PrefetchScalarGridSpec(
        num_scalar_prefetch=2,
        grid=(n // tm,),
        in_specs=[pl.BlockSpec(memory_space=pl.ANY),
                  pl.BlockSpec((tm, d), lambda i, a, c: (i, 0)),
                  pl.BlockSpec((tm, LANES), lambda i, a, c: (i, 0)),
                  pl.BlockSpec((1, d), lambda i, a, c: (0, 0)),
                  pl.BlockSpec((1, d), lambda i, a, c: (0, 0))],
        out_specs=pl.BlockSpec((tm, d), lambda i, a, c: (i, 0)),
        scratch_shapes=[pltpu.VMEM((tm, d), F32), pltpu.VMEM((tm, d), F32), pltpu.SemaphoreType.DMA(())])
    return pl.pallas_call(
        _combine_ln_kernel,
        grid_spec=grid_spec,
        out_shape=jax.ShapeDtypeStruct((n, d), F32),
        compiler_params=_params(("arbitrary",)),
        name="moe_combine_ln",
    )(d1, d2, yb, h, info, g.reshape(1, d), b.reshape(1, d))


def _moe_ln(h, router_w, router_b, w1, w3, w2, g, b, tm):
    n, d = h.shape
    info, counts = _router(h, router_w, router_b, tm)
    counts = counts[0, :N_EXPERTS].astype(I32)
    padded = (counts + MOE_BLK - 1) // MOE_BLK * MOE_BLK
    pad_end = jnp.cumsum(padded)
    pad_start = pad_end - padded
    e1, e2 = info[:, 0].astype(I32), info[:, 1].astype(I32)
    d1 = pad_start[e1] + info[:, 2].astype(I32)
    d2 = pad_start[e2] + info[:, 3].astype(I32)
    nblk = -(-(n * TOP_K) // MOE_BLK) + N_EXPERTS
    blk_expert = jnp.minimum(jnp.searchsorted(pad_end, jnp.arange(nblk, dtype=I32) * MOE_BLK, side='right'),
                             N_EXPERTS - 1).astype(I32)
    tok = jnp.arange(n, dtype=I32)
    slot_tok = jnp.zeros((nblk * MOE_BLK,), I32).at[jnp.concatenate([d1, d2])].set(
        jnp.concatenate([tok, tok]), unique_indices=True)
    n_used = (pad_end[-1:] // MOE_BLK).astype(I32)
    yb = _experts(h, slot_tok, blk_expert, n_used, w1, w3, w2, 512)
    return _combine_ln(yb, d1, d2, h, info, g, b, tm)


def kernel(x, meta_tokens, w_in_even, conv_w, s5_a_re, s5_a_im, s5_log_dt, s5_b_re, s5_b_im, s5_c_re, s5_c_im, s5_d, s5_w_glu, w_out_even, ln_mix_even_g, ln_mix_even_b, ffn_w1, ffn_w3, ffn_w2, ln_ffn_even_g, ln_ffn_even_b, w_in_odd, mlstm_b_i, mlstm_b_f, rel_bias_table, w_out_odd, ln_mix_odd_g, ln_mix_odd_b, router_w, router_b, expert_w1, expert_w3, expert_w2, ln_ffn_odd_g, ln_ffn_odd_b):
    bsz, seq, d = x.shape
    topk = min(TOPK_MAX, seq // 4)
    lh = seq + N_META
    lp = -(-lh // SEQ_TILE) * SEQ_TILE
    n = bsz * lp
    tm = _pick(n, (1280, 640, 512, 256))
    tln = _pick(n, (512, 256))
    tseq = _pick(lp, (1280, 640, 256))

    h = jnp.concatenate([jnp.broadcast_to(meta_tokens.astype(x.dtype)[None], (bsz, N_META, d)), x,
                         jnp.zeros((bsz, lp - lh, d), x.dtype)], axis=1).reshape(n, d)
    hb = h.astype(BF16)

    p = _proj(hb, w_in_even[0].astype(BF16), F32, tm, 1280)
    y_conv = _conv_mixer(p, conv_w[0].astype(F32), lp, tseq, 512)
    ops = _s5_operators(s5_a_re[0], s5_a_im[0], s5_log_dt[0], s5_b_re[0], s5_b_im[0], s5_c_re[0], s5_c_im[0], S5_T)
    u = p[:, 3 * D_CONV:].astype(BF16)
    y_lin = _s5_linear(u, ops, bsz, lp)
    y_s5 = _s5_glu(y_lin, p, s5_d[0], s5_w_glu[0], tm)
    h, hb = _outproj_ln(y_conv, y_s5, w_out_even[0], h, ln_mix_even_g[0], ln_mix_even_b[0], tln)
    h, hb = _ffn_ln(h, ffn_w1[0], ffn_w3[0], ffn_w2[0], ln_ffn_even_g[0], ln_ffn_even_b[0], tln, 512)

    w = w_in_odd[0]
    o = np.cumsum([0] + [D_MLSTM] * 4 + [MLSTM_HEADS] * 2 + [D_DSA] * 3 + [IDX_HEADS * IDX_DH, IDX_DH, IDX_HEADS])
    seg = lambda j: w[:, o[j]:o[j + 1]]
    w_big = jnp.concatenate([seg(0), seg(1), seg(2), seg(3), seg(6), seg(7), seg(8), seg(9)], axis=1)
    w_small = jnp.concatenate([seg(4), seg(5), seg(10), seg(11)], axis=1)
    w_small = jnp.pad(w_small, ((0, 0), (0, LANES - w_small.shape[1])))
    big = _proj(hb, w_big.astype(BF16), BF16, tm, 1280)
    small = _proj(hb, w_small.astype(BF16), F32, tm, LANES)
    gate_bias = jnp.zeros((1, LANES), F32).at[0, :MLSTM_HEADS].set(mlstm_b_i[0].astype(F32))
    gate_bias = gate_bias.at[0, MLSTM_HEADS:2 * MLSTM_HEADS].set(mlstm_b_f[0].astype(F32))
    y_m = _mlstm(big, small, gate_bias, bsz, lp)
    kidx = small[:, 2 * MLSTM_HEADS:2 * MLSTM_HEADS + IDX_DH].astype(BF16).reshape(bsz, lp, IDX_DH).transpose(0, 2, 1)
    widx = small[:, 2 * MLSTM_HEADS + IDX_DH:2 * MLSTM_HEADS + IDX_DH + IDX_HEADS]
    mask = _dsa_select(big, kidx, widx, bsz, lp, topk)
    y_d = _dsa_attention(big, mask, rel_bias_table, bsz, lp)
    h, hb = _outproj_ln(y_m, y_d, w_out_odd[0], h, ln_mix_odd_g[0], ln_mix_odd_b[0], tln)
    h = _moe_ln(h, router_w[0], router_b[0], expert_w1[0], expert_w3[0], expert_w2[0],
                ln_ffn_odd_g[0], ln_ffn_odd_b[0], tln)
    return h.reshape(bsz, lp, d)[:, N_META:lh]
```

```python
import functools
import math

import jax
import jax.numpy as jnp
import numpy as np
from jax import lax
from jax.experimental import pallas as pl
from jax.experimental.pallas import tpu as pltpu

F32 = jnp.float32
BF16 = jnp.bfloat16
I32 = jnp.int32
I16 = jnp.int16

D_MODEL = 2048
DEPTH = 2
CHUNK = 64
N_META = 16
CONV_WIDTH = 3
D_CONV = 1536
S5_CH = 16
S5_STATE = 64
D_SSM = 512
S5_GROUPS = D_SSM // S5_CH
MLSTM_HEADS = 4
MLSTM_DH = 256
D_MLSTM = MLSTM_HEADS * MLSTM_DH
DSA_HEADS = 8
DSA_DH = 128
D_DSA = DSA_HEADS * DSA_DH
IDX_HEADS = 8
IDX_DH = 64
TOPK_MAX = 256
NUM_BUCKETS = 32
MAX_DISTANCE = 128
D_FF = 5632
N_EXPERTS = 8
TOP_K = 2
D_EXPERT = 5632
ALPHA = (2 * DEPTH) ** 0.25
LN_EPS = 1e-5

LANES = 128
SEQ_TILE = 256
S5_T = 64
SEARCH_W = 1024
MOE_BLK = 512
VMEM_LIMIT = 56 * 1024 * 1024
NEG = -1e30
LOG2E = math.log2(math.e)
INT_MIN = -(2 ** 31)


def _params(sem, vmem=VMEM_LIMIT):
    return pltpu.CompilerParams(dimension_semantics=sem, vmem_limit_bytes=vmem)


def _pick(n, candidates):
    for c in candidates:
        if n % c == 0:
            return c
    raise ValueError(f"no tile in {candidates} divides {n}")


def _dot(a, b):
    return jnp.dot(a, b, preferred_element_type=F32)


def _dot_nt(a, b):
    return lax.dot_general(a, b, (((1,), (1,)), ((), ())), preferred_element_type=F32)


def _split3(x):
    hi = x.astype(BF16)
    r = x - hi.astype(F32)
    mid = r.astype(BF16)
    lo = (r - mid.astype(F32)).astype(BF16)
    return hi, mid, lo


def _layer_norm(z, g, b):
    mu = jnp.mean(z, axis=-1, keepdims=True)
    zc = z - mu
    var = jnp.mean(zc * zc, axis=-1, keepdims=True)
    return zc * lax.rsqrt(var + LN_EPS) * g + b


def _proj_kernel(x_ref, w_ref, o_ref):
    o_ref[...] = _dot(x_ref[...], w_ref[...]).astype(o_ref.dtype)


def _proj(x, w, out_dtype, tm, tn):
    n, k = x.shape
    m = w.shape[1]
    return pl.pallas_call(
        _proj_kernel,
        grid=(n // tm, m // tn),
        in_specs=[pl.BlockSpec((tm, k), lambda i, j: (i, 0)),
                  pl.BlockSpec((k, tn), lambda i, j: (0, j))],
        out_specs=pl.BlockSpec((tm, tn), lambda i, j: (i, j)),
        out_shape=jax.ShapeDtypeStruct((n, m), out_dtype),
        compiler_params=_params(("parallel", "parallel")),
        name="proj",
    )(x, w)


def _outproj_ln_kernel(ya_ref, yb_ref, wa_ref, wb_ref, h_ref, g_ref, b_ref, o_ref, ob_ref):
    mix = _dot(ya_ref[...], wa_ref[...]) + _dot(yb_ref[...], wb_ref[...])
    out = _layer_norm(ALPHA * h_ref[...] + mix, g_ref[...], b_ref[...])
    o_ref[...] = out
    ob_ref[...] = out.astype(BF16)


def _outproj_ln(ya, yb, w, h, g, b, tm):
    n, d = h.shape
    ka, kb = ya.shape[1], yb.shape[1]
    wa, wb = w[:ka].astype(BF16), w[ka:].astype(BF16)
    row = lambda i: (i, 0)
    fixed = lambda i: (0, 0)
    return pl.pallas_call(
        _outproj_ln_kernel,
        grid=(n // tm,),
        in_specs=[pl.BlockSpec((tm, ka), row), pl.BlockSpec((tm, kb), row),
                  pl.BlockSpec((ka, d), fixed), pl.BlockSpec((kb, d), fixed),
                  pl.BlockSpec((tm, d), row),
                  pl.BlockSpec((1, d), fixed), pl.BlockSpec((1, d), fixed)],
        out_specs=[pl.BlockSpec((tm, d), row), pl.BlockSpec((tm, d), row)],
        out_shape=[jax.ShapeDtypeStruct((n, d), F32), jax.ShapeDtypeStruct((n, d), BF16)],
        compiler_params=_params(("parallel",)),
        name="outproj_ln",
    )(ya, yb, wa, wb, h, g.reshape(1, d), b.reshape(1, d))


def _conv_kernel(b_ref, c_ref, x_ref, ch_ref, xh_ref, w_ref, y_ref, *, tiles_per_seq):
    i = pl.program_id(0)
    z = c_ref[...] * x_ref[...]
    zh = ch_ref[...] * xh_ref[...]
    zh = jnp.where(i % tiles_per_seq == 0, 0.0, zh)
    r = lax.broadcasted_iota(I32, z.shape, 0)
    z1 = jnp.where(r == 0, zh[7:8], pltpu.roll(z, 1, 0))
    z2 = jnp.where(r == 0, zh[6:7], jnp.where(r == 1, zh[7:8], pltpu.roll(z, 2, 0)))
    w = w_ref[...]
    conv = w[0:1] * z2 + w[1:2] * z1 + w[2:3] * z
    y_ref[...] = (b_ref[...] * conv).astype(y_ref.dtype)


def _conv_mixer(p, conv_w, lp, tm, tc):
    n = p.shape[0]
    ncb = D_CONV // tc
    hb = tm // 8
    body = functools.partial(_conv_kernel, tiles_per_seq=lp // tm)
    return pl.pallas_call(
        body,
        grid=(n // tm, ncb),
        in_specs=[pl.BlockSpec((tm, tc), lambda i, j: (i, j)),
                  pl.BlockSpec((tm, tc), lambda i, j: (i, ncb + j)),
                  pl.BlockSpec((tm, tc), lambda i, j: (i, 2 * ncb + j)),
                  pl.BlockSpec((8, tc), lambda i, j: (jnp.maximum(i * hb - 1, 0), ncb + j)),
                  pl.BlockSpec((8, tc), lambda i, j: (jnp.maximum(i * hb - 1, 0), 2 * ncb + j)),
                  pl.BlockSpec((CONV_WIDTH, tc), lambda i, j: (0, j))],
        out_specs=pl.BlockSpec((tm, tc), lambda i, j: (i, j)),
        out_shape=jax.ShapeDtypeStruct((n, D_CONV), BF16),
        compiler_params=_params(("parallel", "parallel")),
        name="conv_mixer",
    )(p, p, p, p, p, conv_w)


def _s5_operators(a_re, a_im, log_dt, b_re, b_im, c_re, c_im, t):
    a = lax.complex(a_re.astype(F32), a_im.astype(F32))
    dt = jnp.exp(log_dt.astype(F32))[:, None]
    a_bar = jnp.exp(dt * a)
    b_bar = ((a_bar - 1.0) / a)[:, :, None] * lax.complex(b_re.astype(F32), b_im.astype(F32))
    c = lax.complex(c_re.astype(F32), c_im.astype(F32))
    lag = jnp.arange(t + 1, dtype=F32)[:, None, None]
    pw = jnp.exp(lag * (dt * a)[None])
    kern = jnp.real(jnp.einsum('gcp,lgp,gpd->glcd', c, pw[:t], b_bar))
    lagged = jnp.concatenate([jnp.zeros_like(kern), kern], axis=1)
    blocks = jnp.stack([lagged[:, t - j:2 * t - j] for j in range(t)], axis=1)
    g = a.shape[0]
    toep = blocks.transpose(0, 1, 4, 2, 3).reshape(g, t * S5_CH, t * S5_CH)
    wc = pw[:t][::-1].transpose(1, 0, 2)[:, :, None, :] * b_bar.transpose(0, 2, 1)[:, None, :, :]
    wc = wc.reshape(g, t * S5_CH, S5_STATE)
    w_in = jnp.concatenate([jnp.real(wc), jnp.imag(wc)], axis=-1)
    cp = c.transpose(0, 2, 1)[:, :, None, :] * pw[1:t + 1].transpose(1, 2, 0)[:, :, :, None]
    cp = cp.reshape(g, S5_STATE, t * S5_CH)
    v_out = jnp.concatenate([jnp.real(cp), -jnp.imag(cp)], axis=1)
    a_t = jnp.stack([jnp.real(pw[t]), jnp.imag(pw[t])], axis=1)
    return toep.astype(BF16), w_in.astype(BF16), v_out.astype(BF16), a_t


def _s5_kernel(u_ref, toep_ref, win_ref, vout_ref, at_ref, y_ref, er_ref, ei_ref, hr_ref, hi_ref, *, nbatch, nc):
    u = u_ref[...]
    e = _dot(u, win_ref[...])
    er_ref[...] = e[:, :S5_STATE]
    ei_ref[...] = e[:, S5_STATE:]
    at = at_ref[...]
    ar, ai = at[0:1], at[1:2]

    def step(c, carry):
        out = []
        for b in range(nbatch):
            hr, hi = carry[2 * b], carry[2 * b + 1]
            row = b * nc + c
            hr_ref[pl.ds(row, 1), :] = hr
            hi_ref[pl.ds(row, 1), :] = hi
            er = er_ref[pl.ds(row, 1), :]
            ei = ei_ref[pl.ds(row, 1), :]
            out += [ar * hr - ai * hi + er, ar * hi + ai * hr + ei]
        return tuple(out)

    zero = jnp.zeros((1, S5_STATE), F32)
    lax.fori_loop(0, nc, step, (zero,) * (2 * nbatch))
    vout = vout_ref[...]
    y = _dot(u, toep_ref[...])
    for part, rows in ((hr_ref[...], vout[:S5_STATE]), (hi_ref[...], vout[S5_STATE:])):
        hi_, mid_, _ = _split3(part)
        y += _dot(hi_, rows) + _dot(mid_, rows)
    y_ref[...] = y


def _s5_linear(u, ops, nbatch, lp):
    toep, w_in, v_out, a_t = ops
    n = u.shape[0]
    t = S5_T
    nc = lp // t
    rows = nbatch * nc
    width = t * S5_CH
    ug = u.reshape(rows, t, S5_GROUPS, S5_CH).transpose(2, 0, 1, 3).reshape(S5_GROUPS, rows, width)
    grp = lambda g: (g, 0, 0)
    yg = pl.pallas_call(
        functools.partial(_s5_kernel, nbatch=nbatch, nc=nc),
        grid=(S5_GROUPS,),
        in_specs=[pl.BlockSpec((None, rows, width), grp),
                  pl.BlockSpec((None, width, width), grp),
                  pl.BlockSpec((None, width, 2 * S5_STATE), grp),
                  pl.BlockSpec((None, 2 * S5_STATE, width), grp),
                  pl.BlockSpec((None, 2, S5_STATE), grp)],
        out_specs=pl.BlockSpec((None, rows, width), grp),
        out_shape=jax.ShapeDtypeStruct((S5_GROUPS, rows, width), F32),
        scratch_shapes=[pltpu.VMEM((rows, S5_STATE), F32)] * 4,
        compiler_params=_params(("parallel",)),
        name="s5_linear",
    )(ug, toep, w_in, v_out, a_t)
    return yg.reshape(S5_GROUPS, rows, t, S5_CH).transpose(1, 2, 0, 3).reshape(n, D_SSM)


def _s5_glu_kernel(y_ref, u_ref, d_ref, w_ref, o_ref):
    y = y_ref[...] + d_ref[...] * u_ref[...]
    y = jax.nn.gelu(y)
    hi, mid, _ = _split3(y)
    w = w_ref[...]
    gate = _dot(hi, w) + _dot(mid, w)
    o_ref[...] = (y * jax.nn.sigmoid(gate)).astype(o_ref.dtype)


def _s5_glu(y_lin, p, d_skip, w_glu, tm):
    n = y_lin.shape[0]
    ucol = (3 * D_CONV) // D_SSM
    wbd = jnp.einsum('gcd,gh->gdhc', w_glu.astype(F32), jnp.eye(S5_GROUPS, dtype=F32)).reshape(D_SSM, D_SSM)
    return pl.pallas_call(
        _s5_glu_kernel,
        grid=(n // tm,),
        in_specs=[pl.BlockSpec((tm, D_SSM), lambda i: (i, 0)),
                  pl.BlockSpec((tm, D_SSM), lambda i: (i, ucol)),
                  pl.BlockSpec((1, D_SSM), lambda i: (0, 0)),
                  pl.BlockSpec((D_SSM, D_SSM), lambda i: (0, 0))],
        out_specs=pl.BlockSpec((tm, D_SSM), lambda i: (i, 0)),
        out_shape=jax.ShapeDtypeStruct((n, D_SSM), BF16),
        compiler_params=_params(("parallel",)),
        name="s5_glu",
    )(y_lin, p, d_skip.reshape(1, D_SSM).astype(F32), wbd.astype(BF16))


def _ffn_ln_kernel(h_ref, w1_ref, w3_ref, w2_ref, g_ref, b_ref, o_ref, ob_ref, xb_ref, acc_ref, *, nf):
    f = pl.program_id(1)

    @pl.when(f == 0)
    def _():
        xb_ref[...] = h_ref[...].astype(BF16)
        acc_ref[...] = jnp.zeros_like(acc_ref)

    x = xb_ref[...]
    a = _dot(x, w1_ref[...])
    c = _dot(x, w3_ref[...])
    mid = (a * jax.nn.sigmoid(a) * c).astype(BF16)
    acc_ref[...] += _dot(mid, w2_ref[...])

    @pl.when(f == nf - 1)
    def _():
        out = _layer_norm(ALPHA * h_ref[...] + acc_ref[...], g_ref[...], b_ref[...])
        o_ref[...] = out
        ob_ref[...] = out.astype(BF16)


def _ffn_ln(h, w1, w3, w2, g, b, tm, tf):
    n, d = h.shape
    nf = w1.shape[1] // tf
    return pl.pallas_call(
        functools.partial(_ffn_ln_kernel, nf=nf),
        grid=(n // tm, nf),
        in_specs=[pl.BlockSpec((tm, d), lambda i, f: (i, 0)),
                  pl.BlockSpec((d, tf), lambda i, f: (0, f)),
                  pl.BlockSpec((d, tf), lambda i, f: (0, f)),
                  pl.BlockSpec((tf, d), lambda i, f: (f, 0)),
                  pl.BlockSpec((1, d), lambda i, f: (0, 0)),
                  pl.BlockSpec((1, d), lambda i, f: (0, 0))],
        out_specs=[pl.BlockSpec((tm, d), lambda i, f: (i, 0)), pl.BlockSpec((tm, d), lambda i, f: (i, 0))],
        out_shape=[jax.ShapeDtypeStruct((n, d), F32), jax.ShapeDtypeStruct((n, d), BF16)],
        scratch_shapes=[pltpu.VMEM((tm, d), BF16), pltpu.VMEM((tm, d), F32)],
        compiler_params=_params(("parallel", "arbitrary")),
        name="ffn_ln",
    )(h, w1.astype(BF16), w3.astype(BF16), w2.astype(BF16), g.reshape(1, d), b.reshape(1, d))


def _log_sigmoid(x):
    return jnp.minimum(x, 0.0) - jnp.log(1.0 + jnp.exp(-jnp.abs(x)))


def _mlstm_kernel(q_ref, k_ref, v_ref, o_ref, g_ref, gb_ref, y_ref, ct_ref, n_ref, m_ref):
    c = pl.program_id(1)
    t = q_ref.shape[0]
    dh = MLSTM_DH

    @pl.when(c == 0)
    def _():
        ct_ref[...] = jnp.zeros_like(ct_ref)
        n_ref[...] = jnp.zeros_like(n_ref)
        m_ref[...] = jnp.zeros_like(m_ref)

    gates = g_ref[...] + gb_ref[...]
    lf = _log_sigmoid(gates)
    row = lax.broadcasted_iota(I32, (t, t), 0)
    col = lax.broadcasted_iota(I32, (t, t), 1)
    tril = col <= row
    tril_b = jnp.where(tril, 1.0, 0.0).astype(BF16)
    hi, mid, lo = _split3(lf)
    bcum = _dot(tril_b, hi) + _dot(tril_b, mid) + _dot(tril_b, lo)
    gates_t = gates.T
    bcum_t = bcum.T
    for h in range(MLSTM_HEADS):
        sl = slice(h * dh, (h + 1) * dh)
        fh = MLSTM_HEADS + h
        li_row, li_col = gates_t[h:h + 1, :], gates[:, h:h + 1]
        bc_row, bc_col = bcum_t[fh:fh + 1, :], bcum[:, fh:fh + 1]
        m_prev = m_ref[h:h + 1, 0:1]
        dmat = jnp.where(tril, bc_col - bc_row + li_row, -jnp.inf)
        m_inter = bc_col + m_prev
        m_t = jnp.maximum(m_inter, jnp.max(dmat, axis=1, keepdims=True))
        q = q_ref[:, sl]
        k = k_ref[:, sl] * (dh ** -0.5)
        v = v_ref[:, sl]
        s = _dot_nt(q, k) * jnp.exp(dmat - m_t)
        w_inter = jnp.exp(m_inter - m_t)
        ct = ct_ref[h]
        nvec = n_ref[h:h + 1, :]
        num = _dot(s.astype(BF16), v) + w_inter * _dot(q, ct.astype(BF16))
        den = jnp.sum(s, axis=1, keepdims=True) + w_inter * jnp.sum(q.astype(F32) * nvec, axis=1, keepdims=True)
        h_out = num / jnp.maximum(jnp.abs(den), jnp.exp(-m_t))
        y_ref[:, sl] = (jax.nn.sigmoid(o_ref[:, sl].astype(F32)) * h_out).astype(y_ref.dtype)
        b_last = bc_col[t - 1:t, :]
        m_new = jnp.maximum(b_last + m_prev, jnp.max(b_last - bc_row + li_row, axis=1, keepdims=True))
        decay = jnp.exp(b_last + m_prev - m_new)
        kw = k.astype(F32) * jnp.exp(b_last - bc_col + li_col - m_new)
        ct_ref[h] = decay * ct + _dot(kw.T.astype(BF16), v)
        n_ref[h:h + 1, :] = decay * nvec + jnp.sum(kw, axis=0, keepdims=True)
        m_ref[h:h + 1, :] = jnp.broadcast_to(m_new, (1, LANES))


def _mlstm(big, small, gate_bias, nbatch, lp):
    n = big.shape[0]
    t = SEQ_TILE
    nc = lp // t
    blk = lambda j: pl.BlockSpec((t, D_MLSTM), lambda b, c: (b * nc + c, j))
    return pl.pallas_call(
        _mlstm_kernel,
        grid=(nbatch, nc),
        in_specs=[blk(0), blk(1), blk(2), blk(3),
                  pl.BlockSpec((t, LANES), lambda b, c: (b * nc + c, 0)),
                  pl.BlockSpec((1, LANES), lambda b, c: (0, 0))],
        out_specs=pl.BlockSpec((t, D_MLSTM), lambda b, c: (b * nc + c, 0)),
        out_shape=jax.ShapeDtypeStruct((n, D_MLSTM), BF16),
        scratch_shapes=[pltpu.VMEM((MLSTM_HEADS, MLSTM_DH, MLSTM_DH), F32),
                        pltpu.VMEM((8, MLSTM_DH), F32),
                        pltpu.VMEM((8, LANES), F32)],
        compiler_params=_params(("parallel", "arbitrary")),
        name="mlstm",
    )(big, big, big, big, small, gate_bias)


def _chunk_end(pos):
    return jnp.where(pos < N_META, N_META, N_META + CHUNK + CHUNK * ((pos - N_META) >> 6))


def _t5_bucket(rel):
    half = NUM_BUCKETS // 2
    max_exact = half // 2
    ret = jnp.where(rel > 0, half, 0)
    n = jnp.abs(rel)
    nf = jnp.maximum(n, 1).astype(F32)
    large = max_exact + (jnp.log(nf / max_exact) / math.log(MAX_DISTANCE / max_exact) * (half - max_exact)).astype(I32)
    large = jnp.minimum(large, half - 1)
    return ret + jnp.where(n < max_exact, n, large)


def _f32_order_key(x):
    bits = lax.bitcast_convert_type(x, I32)
    return bits ^ ((bits >> 31) & 0x7FFFFFFF)


def _lanes(x, width):
    return jnp.concatenate([x] * (width // LANES), axis=1)


def _dsa_select_kernel(qi_ref, kidx_ref, w_ref, mask_ref, hi_ref, lo_ref, wb_ref, lim_ref, *, lp, topk):
    qb, sw = SEQ_TILE, SEARCH_W
    b = pl.program_id(1)
    nkt = jnp.minimum(b + 2, lp // qb)
    nst = (nkt + sw // qb - 1) // (sw // qb)
    wi = w_ref[...] * (IDX_HEADS ** -0.5) * (IDX_DH ** -0.5)
    for h in range(IDX_HEADS):
        wb_ref[h] = jnp.broadcast_to(wi[:, h:h + 1], (qb, LANES))
    end = _chunk_end(b * qb + lax.broadcasted_iota(I32, (qb, 1), 0))
    lane = lax.broadcasted_iota(I32, (qb, qb), 1)
    low16 = jnp.int16(-(2 ** 15))

    def store_keys(off, key):
        hi_ref[:, pl.ds(off, qb)] = (key >> 16).astype(I16)
        lo_ref[:, pl.ds(off, qb)] = key.astype(I16) ^ low16

    qi = qi_ref[...]
    qi_all = jnp.concatenate([qi[:, h * IDX_DH:(h + 1) * IDX_DH] for h in range(IDX_HEADS)], axis=0)

    def fill(kt, visible_only):
        off = pl.multiple_of(kt * qb, qb)
        s_all = _dot(qi_all, kidx_ref[:, pl.ds(off, qb)])
        score = None
        for h in range(IDX_HEADS):
            term = _lanes(wb_ref[h], qb) * jnp.maximum(s_all[h * qb:(h + 1) * qb], 0.0)
            score = term if score is None else score + term
        key = _f32_order_key(score)
        if visible_only:
            key = jnp.where(off + lane < end, key, INT_MIN)
        store_keys(off, key)

    def loop(lo, hi, body):
        lax.fori_loop(lo, hi, lambda i, c: (body(i), c)[1], 0)

    loop(0, jnp.minimum(b, nkt), lambda kt: fill(kt, False))
    loop(jnp.minimum(b, nkt), nkt, lambda kt: fill(kt, True))
    loop(nkt, nst * (sw // qb), lambda kt: store_keys(pl.multiple_of(kt * qb, qb), jnp.full((qb, qb), INT_MIN, I32)))

    one = jnp.ones((qb, sw), BF16)
    zero = jnp.zeros((qb, sw), BF16)
    ones_rhs = jnp.ones((LANES, LANES), BF16)

    def count(*preds):
        def body(st, accs):
            off = pl.multiple_of(st * sw, sw)
            hi, lo = hi_ref[:, pl.ds(off, sw)], lo_ref[:, pl.ds(off, sw)]
            out = []
            for pred, acc in zip(preds, accs):
                hit = pred(hi, lo, off)
                parts = [hit[:, j * LANES:(j + 1) * LANES] for j in range(sw // LANES)]
                while len(parts) > 1:
                    parts = [parts[j] + parts[j + 1] for j in range(0, len(parts), 2)]
                out.append(acc + parts[0])
            return tuple(out)
        accs = lax.fori_loop(0, nst, body, (jnp.zeros((qb, LANES), BF16),) * len(preds))
        return [_dot(acc, ones_rhs) for acc in accs]

    def as16(u):
        return _lanes((u - 2 ** 15).astype(I16), sw)

    def search(pred_ge, target):
        def rnd(i, u):
            cand = u | jnp.left_shift(jnp.int32(1), 15 - i)
            cnt, = count(pred_ge(as16(cand)))
            return jnp.where(cnt >= target, cand, u)
        return lax.fori_loop(0, 16, rnd, jnp.zeros((qb, LANES), I32))

    kf = jnp.float32(topk)
    t_hi = search(lambda c: lambda hi, lo, off: jnp.where(hi >= c, one, zero), kf)
    thi16 = as16(t_hi)
    above_hi, = count(lambda hi, lo, off: jnp.where(hi > thi16, one, zero))

    def restrict(st):
        off = pl.multiple_of(st * sw, sw)
        lo_ref[:, pl.ds(off, sw)] = jnp.where(hi_ref[:, pl.ds(off, sw)] == thi16, lo_ref[:, pl.ds(off, sw)], low16)

    loop(0, nst, restrict)
    t_lo = search(lambda c: lambda hi, lo, off: jnp.where(lo >= c, one, zero), kf - above_hi)
    tlo16 = as16(t_lo)
    above_lo, equal = count(lambda hi, lo, off: jnp.where(lo > tlo16, one, zero),
                            lambda hi, lo, off: jnp.where(hi == thi16, jnp.where(lo == tlo16, one, zero), zero))
    above = above_hi + above_lo
    lowest = (t_hi == 0) & (t_lo == 0)
    tie = (above + equal > kf) & jnp.logical_not(lowest)
    lim_ref[...] = jnp.full((qb, LANES), lp, I32)

    def col16(off, width):
        return (off + lax.broadcasted_iota(I32, (qb, width), 1)).astype(I16)

    @pl.when(jnp.max(jnp.where(tie, 1, 0)) > 0)
    def _():
        need = kf - above

        def rnd(i, x):
            cand = x | jnp.left_shift(jnp.int32(1), 14 - i)
            c16 = _lanes(cand.astype(I16), sw)
            cnt, = count(lambda hi, lo, off: jnp.where(
                hi == thi16, jnp.where(lo == tlo16, jnp.where(col16(off, sw) < c16, one, zero), zero), zero))
            return jnp.where(cnt < need, cand, x)

        x = lax.fori_loop(0, 15, rnd, jnp.zeros((qb, LANES), I32))
        lim_ref[...] = jnp.where(tie, x, lp)

    thi_t, tlo_t, lim_t = (_lanes(v.astype(I16), qb) for v in (t_hi - 2 ** 15, t_lo - 2 ** 15, lim_ref[...]))
    keep = jnp.zeros((qb, qb), BF16)
    drop = jnp.full((qb, qb), NEG, BF16)

    def emit(kt):
        off = pl.multiple_of(kt * qb, qb)
        hi = hi_ref[:, pl.ds(off, qb)]
        lo = lo_ref[:, pl.ds(off, qb)]
        at_thr = jnp.where(lo == tlo_t, jnp.where(col16(off, qb) <= lim_t, keep, drop), drop)
        in_lo = jnp.where(lo > tlo_t, keep, at_thr)
        mask_ref[:, pl.ds(off, qb)] = jnp.where(hi > thi_t, keep, jnp.where(hi == thi_t, in_lo, drop))

    def blank(kt):
        mask_ref[:, pl.ds(pl.multiple_of(kt * qb, qb), qb)] = drop

    loop(0, nkt, emit)
    loop(nkt, lp // qb, blank)


def _dsa_select(big, kidx, w, nbatch, lp, topk):
    qb = SEQ_TILE
    nqb = lp // qb
    qi_col = big.shape[1] // (IDX_HEADS * IDX_DH) - 1
    lpad = -(-lp // SEARCH_W) * SEARCH_W
    assert lpad <= 256 * LANES
    return pl.pallas_call(
        functools.partial(_dsa_select_kernel, lp=lp, topk=topk),
        grid=(nbatch, nqb),
        in_specs=[pl.BlockSpec((qb, IDX_HEADS * IDX_DH), lambda bb, b: (bb * nqb + b, qi_col)),
                  pl.BlockSpec((None, IDX_DH, lp), lambda bb, b: (bb, 0, 0)),
                  pl.BlockSpec((qb, IDX_HEADS), lambda bb, b: (bb * nqb + b, 0))],
        out_specs=pl.BlockSpec((qb, lp), lambda bb, b: (bb * nqb + b, 0)),
        out_shape=jax.ShapeDtypeStruct((nbatch * lp, lp), BF16),
        scratch_shapes=[pltpu.VMEM((qb, lpad), I16), pltpu.VMEM((qb, lpad), I16),
                        pltpu.VMEM((IDX_HEADS, qb, LANES), F32), pltpu.VMEM((qb, LANES), I32)],
        compiler_params=_params(("parallel", "parallel")),
        name="dsa_select",
    )(big, kidx, w)


def _dsa_attn_kernel(qb_ref, kt_ref, q_ref, k_ref, v_ref, mask_ref, band_ref, far_ref, y_ref,
                     s_ref, m_ref, l_ref, acc_ref, *, lp):
    t = SEQ_TILE
    step = pl.program_id(1)
    b = qb_ref[step]
    kt = kt_ref[step]
    last_kt = jnp.minimum(b + 1, lp // t - 1)

    @pl.when(kt == 0)
    def _():
        m_ref[...] = jnp.full_like(m_ref, NEG)
        l_ref[...] = jnp.zeros_like(l_ref)
        acc_ref[...] = jnp.zeros_like(acc_ref)

    heads = [slice(h * DSA_DH, (h + 1) * DSA_DH) for h in range(DSA_HEADS)]
    near = kt >= b - 1
    mask = mask_ref[...].astype(F32)
    scale = DSA_DH ** -0.5 * LOG2E
    for h, sl in enumerate(heads):
        offset = mask + jnp.where(near, 0.0, far_ref[h])
        s_ref[h] = _dot_nt(q_ref[:, sl], k_ref[:, sl]) * scale + offset

    ones_v = jnp.ones((t, LANES), BF16)

    def update(z):
        m_prev = m_ref[...]
        m_next = jnp.maximum(m_prev, jnp.max(z, axis=2, keepdims=True))
        p = jnp.exp2(z - jnp.concatenate([m_next] * (t // LANES), axis=2)).astype(BF16)
        alpha = jnp.exp2(m_prev - m_next)
        m_ref[...] = m_next
        for h, sl in enumerate(heads):
            pv = _dot(p[h], jnp.concatenate([v_ref[:, sl], ones_v], axis=1))
            acc_ref[:, sl] = alpha[h] * acc_ref[:, sl] + pv[:, :DSA_DH]
            l_ref[h] = alpha[h] * l_ref[h] + pv[:, DSA_DH:]

    @pl.when(near)
    def _():
        update(s_ref[...] + band_ref[kt - b + 1])

    @pl.when(jnp.logical_not(near))
    def _():
        update(s_ref[...])

    @pl.when(kt == last_kt)
    def _():
        for h, sl in enumerate(heads):
            y_ref[:, sl] = (acc_ref[:, sl] / l_ref[h]).astype(y_ref.dtype)


def _dsa_bias_band(rel_bias_table):
    t = SEQ_TILE
    rel = jnp.arange(-2 * t + 1, 2 * t, dtype=I32)
    by_rel = (rel_bias_table.astype(F32)[_t5_bucket(rel)] * LOG2E).T
    bias = jnp.stack([by_rel[:, t - 1 - i:4 * t - 1 - i] for i in range(t)], axis=1)
    i = jnp.arange(t, dtype=I32)[:, None]
    c = jnp.arange(3 * t, dtype=I32)[None, :] - t
    band = jnp.where((c < _chunk_end(i))[None], bias, NEG)
    band = band.reshape(DSA_HEADS, t, 3, t).transpose(2, 0, 1, 3)
    far = rel_bias_table.astype(F32)[_t5_bucket(jnp.int32(-(t + 1)))] * LOG2E
    return band, far


def _dsa_attention(big, mask, rel_bias_table, nbatch, lp):
    t = SEQ_TILE
    nqb = lp // t
    n = big.shape[0]
    pairs = [(b, kt) for b in range(nqb) for kt in range(min(b + 2, nqb))]
    qb_of = jnp.asarray(np.array([p[0] for p in pairs], np.int32))
    kt_of = jnp.asarray(np.array([p[1] for p in pairs], np.int32))
    band, far = _dsa_bias_band(rel_bias_table)
    qcol, kcol, vcol = 4, 5, 6
    qrow = lambda bb, s, qr, kr: (bb * nqb + qr[s], 0)
    grid_spec = pltpu.PrefetchScalarGridSpec(
        num_scalar_prefetch=2,
        grid=(nbatch, len(pairs)),
        in_specs=[pl.BlockSpec((t, D_DSA), lambda bb, s, qr, kr: (bb * nqb + qr[s], qcol)),
                  pl.BlockSpec((t, D_DSA), lambda bb, s, qr, kr: (bb * nqb + kr[s], kcol)),
                  pl.BlockSpec((t, D_DSA), lambda bb, s, qr, kr: (bb * nqb + kr[s], vcol)),
                  pl.BlockSpec((t, t), lambda bb, s, qr, kr: (bb * nqb + qr[s], kr[s])),
                  pl.BlockSpec((3, DSA_HEADS, t, t), lambda bb, s, qr, kr: (0, 0, 0, 0)),
                  pl.BlockSpec(memory_space=pltpu.SMEM)],
        out_specs=pl.BlockSpec((t, D_DSA), qrow),
        scratch_shapes=[pltpu.VMEM((DSA_HEADS, t, t), F32),
                        pltpu.VMEM((DSA_HEADS, t, LANES), F32), pltpu.VMEM((DSA_HEADS, t, LANES), F32),
                        pltpu.VMEM((t, D_DSA), F32)])
    return pl.pallas_call(
        functools.partial(_dsa_attn_kernel, lp=lp),
        grid_spec=grid_spec,
        out_shape=jax.ShapeDtypeStruct((n, D_DSA), BF16),
        compiler_params=_params(("parallel", "arbitrary")),
        name="dsa_attention",
    )(qb_of, kt_of, big, big, big, mask, band, far)


def _router_kernel(h_ref, w_ref, b_ref, info_ref, cnt_ref, carry_ref):
    i = pl.program_id(0)
    tm = h_ref.shape[0]

    @pl.when(i == 0)
    def _():
        carry_ref[...] = jnp.zeros_like(carry_ref)

    xh, xm, _ = _split3(h_ref[...])
    w = w_ref[...]
    wh, wm, _ = _split3(w)
    logits = _dot(xh, wh) + (_dot(xh, wm) + _dot(xm, wh)) + b_ref[...]
    lane = lax.broadcasted_iota(I32, (tm, LANES), 1)
    logits = jnp.where(lane < N_EXPERTS, logits, -jnp.inf)
    v1 = jnp.max(logits, axis=1, keepdims=True)
    e1 = jnp.min(jnp.where(logits == v1, lane, LANES), axis=1, keepdims=True)
    rest = jnp.where(lane == e1, -jnp.inf, logits)
    v2 = jnp.max(rest, axis=1, keepdims=True)
    e2 = jnp.min(jnp.where(rest == v2, lane, LANES), axis=1, keepdims=True)
    ex = jnp.exp(v2 - v1)
    g1 = 1.0 / (1.0 + ex)
    g2 = ex / (1.0 + ex)
    onehot = jnp.where((lane == e1) | (lane == e2), 1.0, 0.0)
    row = lax.broadcasted_iota(I32, (tm, tm), 0)
    col = lax.broadcasted_iota(I32, (tm, tm), 1)
    before = jnp.where(col < row, 1.0, 0.0).astype(BF16)
    rank = _dot(before, onehot.astype(BF16)) + carry_ref[...]
    r1 = jnp.sum(jnp.where(lane == e1, rank, 0.0), axis=1, keepdims=True)
    r2 = jnp.sum(jnp.where(lane == e2, rank, 0.0), axis=1, keepdims=True)
    carry_ref[...] += jnp.sum(onehot, axis=0, keepdims=True)
    cnt_ref[...] = carry_ref[...]
    info = jnp.where(lane == 0, e1.astype(F32), 0.0)
    info = jnp.where(lane == 1, e2.astype(F32), info)
    info = jnp.where(lane == 2, r1, info)
    info = jnp.where(lane == 3, r2, info)
    info = jnp.where(lane == 4, g1, info)
    info = jnp.where(lane == 5, g2, info)
    info_ref[...] = info


def _router(h, router_w, router_b, tm):
    n, d = h.shape
    wpad = jnp.zeros((d, LANES), F32).at[:, :N_EXPERTS].set(router_w.astype(F32))
    bpad = jnp.zeros((1, LANES), F32).at[0, :N_EXPERTS].set(router_b.astype(F32))
    return pl.pallas_call(
        _router_kernel,
        grid=(n // tm,),
        in_specs=[pl.BlockSpec((tm, d), lambda i: (i, 0)),
                  pl.BlockSpec((d, LANES), lambda i: (0, 0)),
                  pl.BlockSpec((1, LANES), lambda i: (0, 0))],
        out_specs=[pl.BlockSpec((tm, LANES), lambda i: (i, 0)), pl.BlockSpec((1, LANES), lambda i: (0, 0))],
        out_shape=[jax.ShapeDtypeStruct((n, LANES), F32), jax.ShapeDtypeStruct((1, LANES), F32)],
        scratch_shapes=[pltpu.VMEM((1, LANES), F32)],
        compiler_params=_params(("arbitrary",)),
        name="router",
    )(h, wpad, bpad)


def _expert_kernel(be_ref, tok_ref, nu_ref, x_ref, w1_ref, w3_ref, w2_ref, y_ref, rows_ref, xb_ref, acc_ref, sem,
                   *, nf, nblk):
    del be_ref
    i = pl.program_id(0)
    f = pl.program_id(1)

    def copy(blk, slot, r):
        src = x_ref.at[pl.ds(tok_ref[blk * MOE_BLK + r], 1)]
        return pltpu.make_async_copy(src, rows_ref.at[slot, pl.ds(r, 1)], sem.at[slot])

    def gather(blk, slot):
        def body(r, carry):
            copy(blk, slot, r).start()
            return carry
        lax.fori_loop(0, MOE_BLK, body, 0, unroll=8)

    def drain(slot):
        def body(r, carry):
            copy(0, slot, 0).wait()
            return carry
        lax.fori_loop(0, MOE_BLK, body, 0, unroll=8)

    @pl.when(f == 0)
    def _():
        slot = i % 2

        @pl.when(i == 0)
        def _():
            gather(0, 0)

        drain(slot)

        @pl.when(i + 1 < nblk)
        def _():
            gather(i + 1, 1 - slot)

        xb_ref[...] = rows_ref[slot].astype(BF16)
        acc_ref[...] = jnp.zeros_like(acc_ref)

    @pl.when(i < nu_ref[0])
    def _():
        x = xb_ref[...]
        a = _dot(x, w1_ref[...])
        c = _dot(x, w3_ref[...])
        mid = (a * jax.nn.sigmoid(a) * c).astype(BF16)
        acc_ref[...] += _dot(mid, w2_ref[...])

    @pl.when(f == nf - 1)
    def _():
        y_ref[...] = acc_ref[...]


def _experts(x, slot_tok, blk_expert, n_used, w1, w3, w2, tf):
    d = x.shape[1]
    cap = slot_tok.shape[0]
    nblk = cap // MOE_BLK
    nf = w1.shape[2] // tf
    wcol = lambda i, f, nu: jnp.where(i < nu[0], f, 0)
    grid_spec = pltpu.PrefetchScalarGridSpec(
        num_scalar_prefetch=3,
        grid=(nblk, nf),
        in_specs=[pl.BlockSpec(memory_space=pl.ANY),
                  pl.BlockSpec((None, d, tf), lambda i, f, be, tk, nu: (be[i], 0, wcol(i, f, nu))),
                  pl.BlockSpec((None, d, tf), lambda i, f, be, tk, nu: (be[i], 0, wcol(i, f, nu))),
                  pl.BlockSpec((None, tf, d), lambda i, f, be, tk, nu: (be[i], wcol(i, f, nu), 0))],
        out_specs=pl.BlockSpec((MOE_BLK, d), lambda i, f, be, tk, nu: (i, 0)),
        scratch_shapes=[pltpu.VMEM((2, MOE_BLK, d), F32), pltpu.VMEM((MOE_BLK, d), BF16),
                        pltpu.VMEM((MOE_BLK, d), F32), pltpu.SemaphoreType.DMA((2,))])
    return pl.pallas_call(
        functools.partial(_expert_kernel, nf=nf, nblk=nblk),
        grid_spec=grid_spec,
        out_shape=jax.ShapeDtypeStruct((cap, d), F32),
        compiler_params=_params(("arbitrary", "arbitrary")),
        name="moe_experts",
    )(blk_expert, slot_tok, n_used, x, w1.astype(BF16), w3.astype(BF16), w2.astype(BF16))


def _combine_ln_kernel(d1_ref, d2_ref, yb_ref, h_ref, info_ref, g_ref, b_ref, o_ref, y1_ref, y2_ref, sem):
    tm = h_ref.shape[0]
    base = pl.program_id(0) * tm

    def copy(dest, buf, r):
        return pltpu.make_async_copy(yb_ref.at[pl.ds(dest, 1)], buf.at[pl.ds(r, 1)], sem)

    def issue(r, carry):
        copy(d1_ref[base + r], y1_ref, r).start(priority=0)
        copy(d2_ref[base + r], y2_ref, r).start(priority=1)
        return carry

    def drain(r, carry):
        copy(0, y1_ref, 0).wait()
        copy(0, y2_ref, 0).wait()
        return carry

    lax.fori_loop(0, tm, issue, 0, unroll=8)
    lax.fori_loop(0, tm, drain, 0, unroll=8)
    info = info_ref[...]
    moe = info[:, 4:5] * y1_ref[...] + info[:, 5:6] * y2_ref[...]
    o_ref[...] = _layer_norm(ALPHA * h_ref[...] + moe, g_ref[...], b_ref[...])


def _combine_ln(yb, d1, d2, h, info, g, b, tm):
    n, d = h.shape
    grid_spec = pltpu.PrefetchScalarGridSpec(
        num_scalar_prefetch=2,
        grid=(n // tm,),
        in_specs=[pl.BlockSpec(memory_space=pl.ANY),
                  pl.BlockSpec((tm, d), lambda i, a, c: (i, 0)),
                  pl.BlockSpec((tm, LANES), lambda i, a, c: (i, 0)),
                  pl.BlockSpec((1, d), lambda i, a, c: (0, 0)),
                  pl.BlockSpec((1, d), lambda i, a, c: (0, 0))],
        out_specs=pl.BlockSpec((tm, d), lambda i, a, c: (i, 0)),
        scratch_shapes=[pltpu.VMEM((tm, d), F32), pltpu.VMEM((tm, d), F32), pltpu.SemaphoreType.DMA(())])
    return pl.pallas_call(
        _combine_ln_kernel,
        grid_spec=grid_spec,
        out_shape=jax.ShapeDtypeStruct((n, d), F32),
        compiler_params=_params(("arbitrary",)),
        name="moe_combine_ln",
    )(d1, d2, yb, h, info, g.reshape(1, d), b.reshape(1, d))


def _moe_ln(h, router_w, router_b, w1, w3, w2, g, b, tm):
    n, d = h.shape
    info, counts = _router(h, router_w, router_b, tm)
    counts = counts[0, :N_EXPERTS].astype(I32)
    padded = (counts + MOE_BLK - 1) // MOE_BLK * MOE_BLK
    pad_end = jnp.cumsum(padded)
    pad_start = pad_end - padded
    e1, e2 = info[:, 0].astype(I32), info[:, 1].astype(I32)
    d1 = pad_start[e1] + info[:, 2].astype(I32)
    d2 = pad_start[e2] + info[:, 3].astype(I32)
    nblk = -(-(n * TOP_K) // MOE_BLK) + N_EXPERTS
    blk_expert = jnp.minimum(jnp.searchsorted(pad_end, jnp.arange(nblk, dtype=I32) * MOE_BLK, side='right'),
                             N_EXPERTS - 1).astype(I32)
    tok = jnp.arange(n, dtype=I32)
    slot_tok = jnp.zeros((nblk * MOE_BLK,), I32).at[jnp.concatenate([d1, d2])].set(
        jnp.concatenate([tok, tok]), unique_indices=True)
    n_used = (pad_end[-1:] // MOE_BLK).astype(I32)
    yb = _experts(h, slot_tok, blk_expert, n_used, w1, w3, w2, 512)
    return _combine_ln(yb, d1, d2, h, info, g, b, tm)


def kernel(x, meta_tokens, w_in_even, conv_w, s5_a_re, s5_a_im, s5_log_dt, s5_b_re, s5_b_im, s5_c_re, s5_c_im, s5_d, s5_w_glu, w_out_even, ln_mix_even_g, ln_mix_even_b, ffn_w1, ffn_w3, ffn_w2, ln_ffn_even_g, ln_ffn_even_b, w_in_odd, mlstm_b_i, mlstm_b_f, rel_bias_table, w_out_odd, ln_mix_odd_g, ln_mix_odd_b, router_w, router_b, expert_w1, expert_w3, expert_w2, ln_ffn_odd_g, ln_ffn_odd_b):
    bsz, seq, d = x.shape
    topk = min(TOPK_MAX, seq // 4)
    lh = seq + N_META
    lp = -(-lh // SEQ_TILE) * SEQ_TILE
    n = bsz * lp
    tm = _pick(n, (1280, 640, 512, 256))
    tln = _pick(n, (512, 256))
    tseq = _pick(lp, (1280, 640, 256))

    h = jnp.concatenate([jnp.broadcast_to(meta_tokens.astype(x.dtype)[None], (bsz, N_META, d)), x,
                         jnp.zeros((bsz, lp - lh, d), x.dtype)], axis=1).reshape(n, d)
    hb = h.astype(BF16)

    p = _proj(hb, w_in_even[0].astype(BF16), F32, tm, 1280)
    y_conv = _conv_mixer(p, conv_w[0].astype(F32), lp, tseq, 512)
    ops = _s5_operators(s5_a_re[0], s5_a_im[0], s5_log_dt[0], s5_b_re[0], s5_b_im[0], s5_c_re[0], s5_c_im[0], S5_T)
    u = p[:, 3 * D_CONV:].astype(BF16)
    y_lin = _s5_linear(u, ops, bsz, lp)
    y_s5 = _s5_glu(y_lin, p, s5_d[0], s5_w_glu[0], tm)
    h, hb = _outproj_ln(y_conv, y_s5, w_out_even[0], h, ln_mix_even_g[0], ln_mix_even_b[0], tln)
    h, hb = _ffn_ln(h, ffn_w1[0], ffn_w3[0], ffn_w2[0], ln_ffn_even_g[0], ln_ffn_even_b[0], tln, 512)

    w = w_in_odd[0]
    o = np.cumsum([0] + [D_MLSTM] * 4 + [MLSTM_HEADS] * 2 + [D_DSA] * 3 + [IDX_HEADS * IDX_DH, IDX_DH, IDX_HEADS])
    seg = lambda j: w[:, o[j]:o[j + 1]]
    w_big = jnp.concatenate([seg(0), seg(1), seg(2), seg(3), seg(6), seg(7), seg(8), seg(9)], axis=1)
    w_small = jnp.concatenate([seg(4), seg(5), seg(10), seg(11)], axis=1)
    w_small = jnp.pad(w_small, ((0, 0), (0, LANES - w_small.shape[1])))
    big = _proj(hb, w_big.astype(BF16), BF16, tm, 1280)
    small = _proj(hb, w_small.astype(BF16), F32, tm, LANES)
    gate_bias = jnp.zeros((1, LANES), F32).at[0, :MLSTM_HEADS].set(mlstm_b_i[0].astype(F32))
    gate_bias = gate_bias.at[0, MLSTM_HEADS:2 * MLSTM_HEADS].set(mlstm_b_f[0].astype(F32))
    y_m = _mlstm(big, small, gate_bias, bsz, lp)
    kidx = small[:, 2 * MLSTM_HEADS:2 * MLSTM_HEADS + IDX_DH].astype(BF16).reshape(bsz, lp, IDX_DH).transpose(0, 2, 1)
    widx = small[:, 2 * MLSTM_HEADS + IDX_DH:2 * MLSTM_HEADS + IDX_DH + IDX_HEADS]
    mask = _dsa_select(big, kidx, widx, bsz, lp, topk)
    y_d = _dsa_attention(big, mask, rel_bias_table, bsz, lp)
    h, hb = _outproj_ln(y_m, y_d, w_out_odd[0], h, ln_mix_odd_g[0], ln_mix_odd_b[0], tln)
    h = _moe_ln(h, router_w[0], router_b[0], expert_w1[0], expert_w3[0], expert_w2[0],
                ln_ffn_odd_g[0], ln_ffn_odd_b[0], tln)
    return h.reshape(bsz, lp, d)[:, N_META:lh]
```
